```python
import jax, jax.numpy as jnp
from jax import lax
import numpy as np

D_MODEL = 1024
BATCH = 8
SEQ = 4096
DEPTH = 2

D_MIX = D_MODEL
HEAD_DIM = 64
D_A = D_MIX // 4
D_B = D_MIX // 4
D_C = D_MIX // 4
D_D = D_MIX - D_A - D_B - D_C
H_B = D_B // HEAD_DIM
H_C = D_C // HEAD_DIM
CONF_KERNEL = 31
LRU_CONV = 4
LRU_C = 8.0
SHORT_CONV = 3
LORA_W = 32
LORA_A = 32
LORA_G = 64
C_COLS = 3 * D_C + LORA_W + LORA_A + LORA_G
P_IN = 2 * D_A + 2 * D_B + C_COLS + 3 * D_D
RWKV_GN_EPS = 64e-5
N_EXPERTS = 64
TOP_K = 8
N_GROUPS = 8
TOPK_GROUPS = 4
E_PER_GROUP = N_EXPERTS // N_GROUPS
D_EXPERT = 256
D_SHARED = 256
ROUTED_SCALE = 2.5
MOE_BLOCK = 128
ALPHA = (2.0 * DEPTH) ** 0.25
BETA = (8.0 * DEPTH) ** -0.25
LN_EPS = 1e-5

kernel_name = "hybrid_conv_lru_rwkv7_shortconv_moe_deepnorm"


def _layer_norm(x, g, b, eps=LN_EPS):
    xf = x.astype(jnp.float32)
    mu = jnp.mean(xf, -1, keepdims=True)
    var = jnp.mean(jnp.square(xf - mu), -1, keepdims=True)
    return ((xf - mu) * lax.rsqrt(var + eps)).astype(x.dtype) * g + b


def _causal_dwconv(x, w):
    k = w.shape[0]
    return lax.conv_general_dilated(
        x, w[:, None, :].astype(x.dtype), window_strides=(1,), padding=[(k - 1, 0)],
        dimension_numbers=("NWC", "WIO", "NWC"), feature_group_count=x.shape[-1])


def _conformer_conv(pa, conv_a, conv_a_bias, ln_a_g, ln_a_b):
    val, gate = jnp.split(pa, 2, axis=-1)
    hid = val * jax.nn.sigmoid(gate)
    hid = _causal_dwconv(hid, conv_a) + conv_a_bias
    return jax.nn.silu(_layer_norm(hid, ln_a_g, ln_a_b))


def _linear_scan(a, bx):
    def combine(left, right):
        a_l, b_l = left
        a_r, b_r = right
        return a_l * a_r, a_r * b_l + b_r
    _, h = lax.associative_scan(combine, (a, bx), axis=1)
    return h


def _rg_lru(pb, conv_b, conv_b_bias, w_rg, b_rg, w_ig, b_ig, lru_lambda):
    bsz, s, _ = pb.shape
    xb, gb = jnp.split(pb, 2, axis=-1)
    u = _causal_dwconv(xb, conv_b) + conv_b_bias
    uh = u.reshape(bsz, s, H_B, HEAD_DIM)
    r = jax.nn.sigmoid((jnp.einsum("bshd,hde->bshe", uh, w_rg).reshape(bsz, s, D_B) + b_rg).astype(jnp.float32))
    i = jax.nn.sigmoid((jnp.einsum("bshd,hde->bshe", uh, w_ig).reshape(bsz, s, D_B) + b_ig).astype(jnp.float32))
    log_a = -LRU_C * r * jax.nn.softplus(-lru_lambda.astype(jnp.float32))
    a = jnp.exp(log_a)
    mult = jnp.sqrt(-jnp.expm1(2.0 * log_a))
    h = _linear_scan(a, mult * (i * u.astype(jnp.float32)))
    return h.astype(pb.dtype) * jax.nn.gelu(gb, approximate=True)


def _rwkv7_scan(r, decay, k, v, kk, a):
    bsz, _, h, n = r.shape

    def step(state, inp):
        r_t, w_t, k_t, v_t, kk_t, a_t = inp
        s_kk = jnp.einsum("bhij,bhj->bhi", state, kk_t)
        state = (state * w_t[:, :, None, :]
                 - s_kk[..., None] * (kk_t * a_t)[:, :, None, :]
                 + v_t[..., None] * k_t[:, :, None, :])
        return state, jnp.einsum("bhij,bhj->bhi", state, r_t)

    seq_major = tuple(jnp.moveaxis(t, 1, 0) for t in (r, decay, k, v, kk, a))
    state0 = jnp.zeros((bsz, h, n, n), jnp.float32)
    _, y = lax.scan(step, state0, seq_major)
    return jnp.moveaxis(y, 0, 1)


def _rwkv7_mix(pc, mu_c, w0, w_w2, a0, w_a2, w_g2, k_k, k_a, r_k, gn_g, gn_b):
    bsz, s, _ = pc.shape
    f32 = jnp.float32
    prev = jnp.pad(pc, ((0, 0), (1, 0), (0, 0)))[:, :-1]
    xs = pc + (prev - pc) * mu_c
    r, k, v, xw, xa, xg = jnp.split(
        xs, [D_C, 2 * D_C, 3 * D_C, 3 * D_C + LORA_W, 3 * D_C + LORA_W + LORA_A], axis=-1)
    wlog = -jax.nn.softplus(-(w0 + jnp.tanh(xw) @ w_w2).astype(f32)) - 0.5
    decay = jnp.exp(-jnp.exp(wlog))
    a = jax.nn.sigmoid((a0 + xa @ w_a2).astype(f32))
    g = (jax.nn.sigmoid(xg) @ w_g2).astype(f32)
    kf = k.astype(f32)
    kk = (kf * k_k).reshape(bsz, s, H_C, HEAD_DIM)
    kk = kk * lax.rsqrt(jnp.sum(kk * kk, -1, keepdims=True) + 1e-12)
    kf = kf * (1.0 + (a - 1.0) * k_a)
    rh = r.astype(f32).reshape(bsz, s, H_C, HEAD_DIM)
    kh = kf.reshape(bsz, s, H_C, HEAD_DIM)
    vh = v.astype(f32).reshape(bsz, s, H_C, HEAD_DIM)
    y = _rwkv7_scan(rh, decay.reshape(bsz, s, H_C, HEAD_DIM), kh, vh, kk,
                    a.reshape(bsz, s, H_C, HEAD_DIM))
    mu = jnp.mean(y, -1, keepdims=True)
    var = jnp.mean(jnp.square(y - mu), -1, keepdims=True)
    y = ((y - mu) * lax.rsqrt(var + RWKV_GN_EPS)).reshape(bsz, s, D_C) * gn_g + gn_b
    bonus = jnp.sum(rh * kh * r_k, -1, keepdims=True) * vh
    y = (y + bonus.reshape(bsz, s, D_C)) * g
    return y.astype(pc.dtype)


def _short_conv(pd, conv_d):
    gb, gc, hd = jnp.split(pd, 3, axis=-1)
    return gb * _causal_dwconv(gc * hd, conv_d)


def _token_mixers(u, w_in, w_out, conv_a, conv_a_bias, ln_a_g, ln_a_b, conv_b, conv_b_bias,
                  w_rg, b_rg, w_ig, b_ig, lru_lambda, mu_c, w0, w_w2, a0, w_a2, w_g2,
                  k_k, k_a, r_k, gn_g, gn_b, conv_d):
    proj = u @ w_in
    pa, pb, pc, pd = jnp.split(
        proj, [2 * D_A, 2 * D_A + 2 * D_B, 2 * D_A + 2 * D_B + C_COLS], axis=-1)
    y_a = _conformer_conv(pa, conv_a, conv_a_bias, ln_a_g, ln_a_b)
    y_b = _rg_lru(pb, conv_b, conv_b_bias, w_rg, b_rg, w_ig, b_ig, lru_lambda)
    y_c = _rwkv7_mix(pc, mu_c, w0, w_w2, a0, w_a2, w_g2, k_k, k_a, r_k, gn_g, gn_b)
    y_d = _short_conv(pd, conv_d)
    y = jnp.concatenate([y_a, y_b, y_c, y_d], axis=-1)
    return y @ w_out


def _moe_ffn(u, w_router, e_bias, w13, w2, ws13, ws2):
    bsz, s, d = u.shape
    n = bsz * s
    h = u.reshape(n, d)
    scores = jax.nn.sigmoid((h @ w_router).astype(jnp.float32))
    biased = scores + e_bias.astype(jnp.float32)
    grp = biased.reshape(n, N_GROUPS, E_PER_GROUP)
    grp_score = jnp.sum(lax.top_k(grp, 2)[0], axis=-1)
    _, top_grp = lax.top_k(grp_score, TOPK_GROUPS)
    keep = jnp.sum(jax.nn.one_hot(top_grp, N_GROUPS, dtype=jnp.int32), axis=-2) > 0
    masked = jnp.where(jnp.repeat(keep, E_PER_GROUP, axis=-1), biased, -jnp.inf)
    _, idx = lax.top_k(masked, TOP_K)
    wts = jnp.take_along_axis(scores, idx, axis=-1)
    wts = wts / jnp.sum(wts, -1, keepdims=True) * ROUTED_SCALE
    nk = n * TOP_K
    flat_e = idx.reshape(nk)
    flat_w = wts.reshape(nk)
    flat_tok = jnp.arange(nk, dtype=jnp.int32) // TOP_K
    order = jnp.argsort(flat_e)
    se = flat_e[order]
    counts = jnp.bincount(flat_e, length=N_EXPERTS)
    padded = (counts + MOE_BLOCK - 1) // MOE_BLOCK * MOE_BLOCK
    pad_end = jnp.cumsum(padded)
    pad_start = pad_end - padded
    start = jnp.cumsum(counts) - counts
    dest = pad_start[se] + jnp.arange(nk, dtype=jnp.int32) - start[se]
    n_blocks = (nk + N_EXPERTS * (MOE_BLOCK - 1) + MOE_BLOCK - 1) // MOE_BLOCK
    cap = n_blocks * MOE_BLOCK
    buf_tok = jnp.full((cap,), n, jnp.int32).at[dest].set(flat_tok[order])
    buf_w = jnp.zeros((cap,), jnp.float32).at[dest].set(flat_w[order])
    block_start = jnp.arange(n_blocks, dtype=jnp.int32) * MOE_BLOCK
    block_e = jnp.minimum(jnp.searchsorted(pad_end, block_start, side="right"), N_EXPERTS - 1)
    h_pad = jnp.concatenate([h, jnp.zeros((1, d), h.dtype)], axis=0)

    def body(acc, blk):
        tok, wt, e = blk
        xb = jnp.take(h_pad, tok, axis=0)
        gate, up = jnp.split(xb @ w13[e], 2, axis=-1)
        y = (jax.nn.silu(gate) * up) @ w2[e]
        return acc.at[tok].add(y.astype(jnp.float32) * wt[:, None]), None

    acc0 = jnp.zeros((n + 1, d), jnp.float32)
    acc, _ = lax.scan(body, acc0, (buf_tok.reshape(n_blocks, MOE_BLOCK),
                                   buf_w.reshape(n_blocks, MOE_BLOCK), block_e))
    sg, su = jnp.split(h @ ws13, 2, axis=-1)
    shared = (jax.nn.silu(sg) * su) @ ws2
    return (acc[:n].astype(u.dtype) + shared).reshape(bsz, s, d)


def setup_inputs(seed: int = 0) -> dict:
    key = jax.random.key(seed)
    ks = iter(jax.random.split(key, 64))
    L = DEPTH

    def nrm(shape, scale):
        return jax.random.normal(next(ks), shape, jnp.float32) * scale

    def near_one(shape):
        return 1.0 + nrm(shape, 0.02)

    out = {}
    out["x"] = nrm((BATCH, SEQ, D_MODEL), 1.0)
    out["c"] = nrm((BATCH, D_MODEL), 1.0)
    out["w_mod"] = nrm((L, D_MODEL, 6 * D_MODEL), 0.1 * D_MODEL ** -0.5)
    out["b_mod"] = nrm((L, 6 * D_MODEL), 0.01)
    out["w_in"] = nrm((L, D_MODEL, P_IN), D_MODEL ** -0.5)
    out["w_out"] = nrm((L, D_MIX, D_MODEL), BETA * D_MIX ** -0.5)
    out["conv_a"] = nrm((L, CONF_KERNEL, D_A), CONF_KERNEL ** -0.5)
    out["conv_a_bias"] = nrm((L, D_A), 0.01)
    out["ln_a_g"] = near_one((L, D_A))
    out["ln_a_b"] = nrm((L, D_A), 0.01)
    out["conv_b"] = nrm((L, LRU_CONV, D_B), LRU_CONV ** -0.5)
    out["conv_b_bias"] = nrm((L, D_B), 0.01)
    out["w_rg"] = nrm((L, H_B, HEAD_DIM, HEAD_DIM), HEAD_DIM ** -0.5)
    out["b_rg"] = nrm((L, D_B), 0.01)
    out["w_ig"] = nrm((L, H_B, HEAD_DIM, HEAD_DIM), HEAD_DIM ** -0.5)
    out["b_ig"] = nrm((L, D_B), 0.01)
    ua = jax.random.uniform(next(ks), (L, D_B), jnp.float32, 0.9, 0.999)
    sa = ua ** (1.0 / LRU_C)
    out["lru_lambda"] = jnp.log(sa) - jnp.log1p(-sa)
    out["mu_c"] = jax.random.uniform(next(ks), (L, C_COLS), jnp.float32)
    out["w0"] = jax.random.uniform(next(ks), (L, D_C), jnp.float32, -6.5, -1.5)
    out["w_w2"] = nrm((L, LORA_W, D_C), 0.1 * LORA_W ** -0.5)
    out["a0"] = nrm((L, D_C), 0.1)
    out["w_a2"] = nrm((L, LORA_A, D_C), 0.1 * LORA_A ** -0.5)
    out["w_g2"] = nrm((L, LORA_G, D_C), LORA_G ** -0.5)
    out["k_k"] = 0.85 + nrm((L, D_C), 0.02)
    out["k_a"] = near_one((L, D_C))
    out["r_k"] = nrm((L, H_C, HEAD_DIM), 0.1)
    out["gn_g"] = near_one((L, D_C))
    out["gn_b"] = nrm((L, D_C), 0.01)
    out["conv_d"] = nrm((L, SHORT_CONV, D_D), SHORT_CONV ** -0.5)
    out["ln1_g"] = near_one((L, D_MODEL))
    out["ln1_b"] = nrm((L, D_MODEL), 0.01)
    out["w_router"] = nrm((L, D_MODEL, N_EXPERTS), D_MODEL ** -0.5)
    out["e_bias"] = nrm((L, N_EXPERTS), 0.01)
    out["w13"] = nrm((L, N_EXPERTS, D_MODEL, 2 * D_EXPERT), D_MODEL ** -0.5)
    out["w2"] = nrm((L, N_EXPERTS, D_EXPERT, D_MODEL), BETA * D_EXPERT ** -0.5)
    out["ws13"] = nrm((L, D_MODEL, 2 * D_SHARED), D_MODEL ** -0.5)
    out["ws2"] = nrm((L, D_SHARED, D_MODEL), BETA * D_SHARED ** -0.5)
    out["ln2_g"] = near_one((L, D_MODEL))
    out["ln2_b"] = nrm((L, D_MODEL), 0.01)
    return out


def reference(x, c, w_mod, b_mod, w_in, w_out, conv_a, conv_a_bias, ln_a_g, ln_a_b,
              conv_b, conv_b_bias, w_rg, b_rg, w_ig, b_ig, lru_lambda, mu_c, w0, w_w2,
              a0, w_a2, w_g2, k_k, k_a, r_k, gn_g, gn_b, conv_d, ln1_g, ln1_b,
              w_router, e_bias, w13, w2, ws13, ws2, ln2_g, ln2_b):
    c_act = jax.nn.silu(c)
    for l in range(DEPTH):
        mod = (c_act @ w_mod[l] + b_mod[l])[:, None, :]
        sh1, sc1, g1, sh2, sc2, g2 = jnp.split(mod, 6, axis=-1)
        u = x * (1.0 + sc1) + sh1
        mix = _token_mixers(u, w_in[l], w_out[l], conv_a[l], conv_a_bias[l], ln_a_g[l], ln_a_b[l],
                            conv_b[l], conv_b_bias[l], w_rg[l], b_rg[l], w_ig[l], b_ig[l],
                            lru_lambda[l], mu_c[l], w0[l], w_w2[l], a0[l], w_a2[l], w_g2[l],
                            k_k[l], k_a[l], r_k[l], gn_g[l], gn_b[l], conv_d[l])
        x = _layer_norm(ALPHA * x + (1.0 + g1) * mix, ln1_g[l], ln1_b[l])
        u = x * (1.0 + sc2) + sh2
        ffn = _moe_ffn(u, w_router[l], e_bias[l], w13[l], w2[l], ws13[l], ws2[l])
        x = _layer_norm(ALPHA * x + (1.0 + g2) * ffn, ln2_g[l], ln2_b[l])
    return x
```

```python
import functools

import jax
import jax.numpy as jnp
from jax import lax
from jax.experimental import pallas as pl
from jax.experimental.pallas import tpu as pltpu

D_MODEL = 1024
HEAD_DIM = 64
D_GRP = 256
N_HEADS = D_GRP // HEAD_DIM
CONF_KERNEL = 31
LRU_CONV = 4
LRU_C = 8.0
SHORT_CONV = 3
LORA_W, LORA_A, LORA_G = 32, 32, 64
C_COLS = 3 * D_GRP + LORA_W + LORA_A + LORA_G
P_IN = 4 * D_GRP + C_COLS + 3 * D_GRP
RWKV_GN_EPS = 64e-5
N_EXPERTS = 64
TOP_K = 8
N_GROUPS = 8
TOPK_GROUPS = 4
E_PER_GROUP = N_EXPERTS // N_GROUPS
D_EXPERT = 256
ROUTED_SCALE = 2.5
LN_EPS = 1e-5

OFF_A = 0
OFF_B = 2 * D_GRP
OFF_C = 4 * D_GRP
OFF_D = OFF_C + C_COLS

SUBLANES = 8
LANES = 128
VMEM_LIMIT_BYTES = 56 * 1024 * 1024

RWKV_CHUNK = 64
CONF_HIST = 32
SMALL_HIST = 8

(V_CONV_A_BIAS, V_LN_A_G, V_LN_A_B, V_CONV_B_BIAS, V_B_RG, V_B_IG, V_LRU_LAMBDA, V_W0, V_A0,
 V_K_K, V_K_A, V_R_K, V_GN_G, V_GN_B) = range(14)
N_VEC_ROWS = 16
CW_A, CW_B, CW_D = 0, CONF_KERNEL, CONF_KERNEL + LRU_CONV
N_CW_ROWS = 40


def _bf16(x):
    return x.astype(jnp.bfloat16)


def _dot(a, b, dims=((1,), (0,))):
    return lax.dot_general(_bf16(a), _bf16(b), (dims, ((), ())),
                           preferred_element_type=jnp.float32)


def _split(a):
    hi = _bf16(a)
    lo = _bf16(a - hi.astype(jnp.float32))
    return hi, lo


def _dot_f32(a, b, dims=((1,), (0,))):
    a_hi, a_lo = _split(a)
    b_hi, b_lo = _split(b)
    dn = (dims, ((), ()))
    f = functools.partial(lax.dot_general, dimension_numbers=dn,
                          preferred_element_type=jnp.float32)
    return f(a_hi, b_hi) + (f(a_hi, b_lo) + f(a_lo, b_hi))


def _dot_lhs_f32(a, b_bf16):
    a_hi, a_lo = _split(a)
    f = functools.partial(jnp.dot, preferred_element_type=jnp.float32)
    return f(a_hi, b_bf16) + f(a_lo, b_bf16)


def _layer_norm(x, g, b):
    mu = jnp.mean(x, axis=-1, keepdims=True)
    xc = x - mu
    var = jnp.mean(xc * xc, axis=-1, keepdims=True)
    return xc * lax.rsqrt(var + LN_EPS) * g + b


def _softplus(x):
    return jnp.maximum(x, 0.0) + jnp.log1p(jnp.exp(-jnp.abs(x)))


def _row_iota(shape):
    return lax.broadcasted_iota(jnp.int32, shape, 0)


def _shift_rows(x, d, fill):
    rolled = pltpu.roll(x, d, axis=0)
    return jnp.where(_row_iota(x.shape) >= d, rolled, fill)


def _mod_kernel(c_ref, w_ref, b_ref, o_ref):
    c = c_ref[...]
    c_act = c * jax.nn.sigmoid(c)
    o_ref[0] = _dot_f32(c_act, w_ref[0]) + b_ref[0]


def _modulation(c, w_mod, b_mod):
    n_layers, d, d6 = w_mod.shape
    bsz = c.shape[0]
    tn = D_MODEL
    return pl.pallas_call(
        _mod_kernel,
        grid=(n_layers, d6 // tn),
        in_specs=[
            pl.BlockSpec((bsz, d), lambda l, j: (0, 0)),
            pl.BlockSpec((1, d, tn), lambda l, j: (l, 0, j)),
            pl.BlockSpec((1, 1, tn), lambda l, j: (l, 0, j)),
        ],
        out_specs=pl.BlockSpec((1, bsz, tn), lambda l, j: (l, 0, j)),
        out_shape=jax.ShapeDtypeStruct((n_layers, bsz, d6), jnp.float32),
        compiler_params=pltpu.CompilerParams(
            dimension_semantics=("arbitrary", "arbitrary"),
            vmem_limit_bytes=VMEM_LIMIT_BYTES),
        name="adaln_modulation",
    )(c, w_mod, b_mod.reshape(n_layers, 1, d6))


def _causal_conv(buf_ref, hist, ts, w_rows, n_taps):
    acc = None
    for k in range(n_taps):
        off = hist - (n_taps - 1) + k
        term = w_rows[k:k + 1, :] * buf_ref[pl.ds(off, ts), :]
        acc = term if acc is None else acc + term
    return acc


def _head_sum(x, bd_ref):
    return _dot_lhs_f32(x, bd_ref[...])


def _mixer_kernel(x_ref, sh_ref, sc_ref, g_ref, w_in_ref, w_out_ref, cw_ref, vec_ref, mu_ref,
                  ln_ref, wgate_ref, wlora_ref, bd_ref, o_ref,
                  hbuf, xbbuf, zbuf, pc_last, lru_h, state,
                  bt_s, rt_s, at_s, kt_s, ab_s, kb_s, v_s, cl_s, y_s, ymix):
    ts = x_ref.shape[1]
    s_idx = pl.program_id(1)

    @pl.when(s_idx == 0)
    def _reset():
        hbuf[pl.ds(0, CONF_HIST), :] = jnp.zeros((CONF_HIST, D_GRP), jnp.float32)
        xbbuf[pl.ds(0, SMALL_HIST), :] = jnp.zeros((SMALL_HIST, D_GRP), jnp.float32)
        zbuf[pl.ds(0, SMALL_HIST), :] = jnp.zeros((SMALL_HIST, D_GRP), jnp.float32)
        pc_last[...] = jnp.zeros(pc_last.shape, jnp.float32)
        lru_h[...] = jnp.zeros(lru_h.shape, jnp.float32)
        state[...] = jnp.zeros(state.shape, jnp.float32)

    x = x_ref[0]
    u = x * (1.0 + sc_ref[0]) + sh_ref[0]
    proj = jnp.dot(_bf16(u), w_in_ref[...], preferred_element_type=jnp.float32)

    vec = vec_ref[...]

    def vrow(i):
        return vec[i:i + 1, :]

    cw = cw_ref[...]

    val = proj[:, OFF_A:OFF_A + D_GRP]
    gate = proj[:, OFF_A + D_GRP:OFF_A + 2 * D_GRP]
    hbuf[pl.ds(CONF_HIST, ts), :] = val * jax.nn.sigmoid(gate)
    conv = _causal_conv(hbuf, CONF_HIST, ts, cw[CW_A:CW_A + CONF_KERNEL], CONF_KERNEL)
    conv = conv + vrow(V_CONV_A_BIAS)
    hbuf[pl.ds(0, CONF_HIST), :] = hbuf[pl.ds(ts, CONF_HIST), :]
    ln_a = _layer_norm(conv, vrow(V_LN_A_G), vrow(V_LN_A_B))
    ymix[:, 0:D_GRP] = ln_a * jax.nn.sigmoid(ln_a)

    xbbuf[pl.ds(SMALL_HIST, ts), :] = proj[:, OFF_B:OFF_B + D_GRP]
    gb = proj[:, OFF_B + D_GRP:OFF_B + 2 * D_GRP]
    ub = _causal_conv(xbbuf, SMALL_HIST, ts, cw[CW_B:CW_B + LRU_CONV], LRU_CONV)
    ub = ub + vrow(V_CONV_B_BIAS)
    xbbuf[pl.ds(0, SMALL_HIST), :] = xbbuf[pl.ds(ts, SMALL_HIST), :]
    gates = jnp.dot(_bf16(ub), wgate_ref[...], preferred_element_type=jnp.float32)
    r_gate = jax.nn.sigmoid(gates[:, :D_GRP] + vrow(V_B_RG))
    i_gate = jax.nn.sigmoid(gates[:, D_GRP:] + vrow(V_B_IG))
    log_a = (-LRU_C) * r_gate * _softplus(-vrow(V_LRU_LAMBDA))
    a_sc = jnp.exp(log_a)
    b_sc = jnp.sqrt(-jnp.tanh(log_a) * (a_sc * a_sc + 1.0)) * (i_gate * ub)
    d = 1
    while d < ts:
        a_sh = _shift_rows(a_sc, d, 1.0)
        b_sh = _shift_rows(b_sc, d, 0.0)
        b_sc = a_sc * b_sh + b_sc
        a_sc = a_sc * a_sh
        d *= 2
    h = b_sc + a_sc * lru_h[0:1, :]
    lru_h[0:1, :] = h[ts - 1:ts, :]
    ymix[:, D_GRP:2 * D_GRP] = h * jax.nn.gelu(gb, approximate=True)

    gbd = proj[:, OFF_D:OFF_D + D_GRP]
    zbuf[pl.ds(SMALL_HIST, ts), :] = (proj[:, OFF_D + D_GRP:OFF_D + 2 * D_GRP]
                                      * proj[:, OFF_D + 2 * D_GRP:OFF_D + 3 * D_GRP])
    convd = _causal_conv(zbuf, SMALL_HIST, ts, cw[CW_D:CW_D + SHORT_CONV], SHORT_CONV)
    zbuf[pl.ds(0, SMALL_HIST), :] = zbuf[pl.ds(ts, SMALL_HIST), :]
    ymix[:, 3 * D_GRP:4 * D_GRP] = gbd * convd

    pc = proj[:, OFF_C:OFF_C + C_COLS]
    prev = jnp.where(_row_iota(pc.shape) == 0, pc_last[0:1, :], pltpu.roll(pc, 1, axis=0))
    pc_last[0:1, :] = pc[ts - 1:ts, :]
    xs = pc + (prev - pc) * mu_ref[...]
    r = xs[:, 0:D_GRP]
    k = xs[:, D_GRP:2 * D_GRP]
    v = xs[:, 2 * D_GRP:3 * D_GRP]
    z = xs[:, 3 * D_GRP:C_COLS]
    lane = lax.broadcasted_iota(jnp.int32, z.shape, 1)
    z_act = jnp.where(lane < LORA_W, jnp.tanh(z),
                      jnp.where(lane < LORA_W + LORA_A, z, jax.nn.sigmoid(z)))
    lora = jnp.dot(_bf16(z_act), wlora_ref[...], preferred_element_type=jnp.float32)
    wlog = -_softplus(-(vrow(V_W0) + lora[:, 0:D_GRP])) - 0.5
    lw = -jnp.exp(wlog)
    a_icl = jax.nn.sigmoid(vrow(V_A0) + lora[:, D_GRP:2 * D_GRP])
    g_out = lora[:, 2 * D_GRP:3 * D_GRP]

    kk = k * vrow(V_K_K)
    kk = kk * lax.rsqrt(_head_sum(kk * kk, bd_ref) + 1e-12)
    kf = k * (1.0 + (a_icl - 1.0) * vrow(V_K_A))

    cl = lw
    seg_row = _row_iota(cl.shape) % RWKV_CHUNK
    d = 1
    while d < RWKV_CHUNK:
        cl = cl + jnp.where(seg_row >= d, pltpu.roll(cl, d, axis=0), 0.0)
        d *= 2
    cl_s[...] = cl
    g_inc = jnp.exp(cl)
    g_inv = jnp.exp(-cl)
    bt_s[...] = kk * jnp.exp(cl - lw)
    rt_s[...] = r * g_inc
    alpha = -(kk * a_icl)
    at_s[...] = alpha * g_inv
    kt_s[...] = kf * g_inv
    v_s[...] = v

    n_chunks = ts // RWKV_CHUNK
    tri_r = lax.broadcasted_iota(jnp.int32, (RWKV_CHUNK, RWKV_CHUNK), 0)
    tri_c = lax.broadcasted_iota(jnp.int32, (RWKV_CHUNK, RWKV_CHUNK), 1)
    strict = tri_r > tri_c
    incl = tri_r >= tri_c
    eye = (tri_r == tri_c).astype(jnp.float32)
    nt = ((1,), (1,))
    tn = ((0,), (0,))

    def chunk_body(c, carry):
        row0 = pl.multiple_of(c * RWKV_CHUNK, RWKV_CHUNK)
        rows = pl.ds(row0, RWKV_CHUNK)
        bt, rt, at, kt, vv = bt_s[rows, :], rt_s[rows, :], at_s[rows, :], kt_s[rows, :], v_s[rows, :]
        clc = cl_s[rows, :]
        cl_end = cl_s[pl.ds(row0 + RWKV_CHUNK - 1, 1), :]
        tail = jnp.exp(cl_end - clc)
        ab = at * jnp.exp(clc) * tail
        kb = kt * jnp.exp(clc) * tail
        g_end = jnp.exp(cl_end)
        y_parts = []
        for hd in range(N_HEADS):
            sl = slice(hd * HEAD_DIM, (hd + 1) * HEAD_DIM)
            b_h, r_h, a_h, k_h, v_h = bt[:, sl], rt[:, sl], at[:, sl], kt[:, sl], vv[:, sl]
            s0 = state[hd]
            lhs = jnp.concatenate([b_h, r_h], axis=0)
            rhs = jnp.concatenate([a_h, k_h], axis=0)
            amat = _dot(lhs, rhs, nt)
            a_ba = jnp.where(strict, amat[:RWKV_CHUNK, :RWKV_CHUNK], 0.0)
            a_bk = jnp.where(strict, amat[:RWKV_CHUNK, RWKV_CHUNK:], 0.0)
            a_ra = jnp.where(incl, amat[RWKV_CHUNK:, :RWKV_CHUNK], 0.0)
            a_rk = jnp.where(incl, amat[RWKV_CHUNK:, RWKV_CHUNK:], 0.0)
            tinv = eye + a_ba
            pw = a_ba
            step = 2
            while step < RWKV_CHUNK:
                pw = _dot(pw, pw)
                tinv = tinv + _dot(pw, tinv)
                step *= 2
            rhs_t = jnp.concatenate([b_h, _dot(a_bk, v_h)], axis=1)
            tx = _dot(tinv, rhs_t)
            w_t, u_t = tx[:, :HEAD_DIM], tx[:, HEAD_DIM:]
            u_h = _dot(w_t, s0, nt) + u_t
            y_h = _dot(r_h, s0, nt) + _dot(a_ra, u_h) + _dot(a_rk, v_h)
            s_new = (s0 * g_end[:, sl] + _dot(u_h, ab[:, sl], tn) + _dot(v_h, kb[:, sl], tn))
            state[hd] = s_new
            y_parts.append(y_h)
        y_s[rows, :] = jnp.concatenate(y_parts, axis=1)
        return carry

    lax.fori_loop(0, n_chunks, chunk_body, 0)

    y = y_s[...]
    inv_n = 1.0 / HEAD_DIM
    mu_y = _head_sum(y, bd_ref) * inv_n
    yc = y - mu_y
    var_y = _head_sum(yc * yc, bd_ref) * inv_n
    y = yc * lax.rsqrt(var_y + RWKV_GN_EPS) * vrow(V_GN_G) + vrow(V_GN_B)
    bonus = _head_sum(r * kf * vrow(V_R_K), bd_ref) * v
    ymix[:, 2 * D_GRP:3 * D_GRP] = (y + bonus) * g_out

    mix = jnp.dot(_bf16(ymix[...]), w_out_ref[...], preferred_element_type=jnp.float32)
    alpha_dn = ln_ref[2:3, :]
    res = alpha_dn * x + (1.0 + g_ref[0]) * mix
    o_ref[0] = _layer_norm(res, ln_ref[0:1, :], ln_ref[1:2, :])


def _token_mixer_layer(x, sh, sc, g, w_in, w_out, cw, vec, mu, ln, wgate, wlora, bd, ts):
    bsz, s, d = x.shape
    grid = (bsz, s // ts)
    full2 = lambda b, i: (0, 0)
    mod_spec = pl.BlockSpec((1, 1, d), lambda b, i: (b, 0, 0))
    f32 = jnp.float32

    def tile(n_cols):
        return pltpu.VMEM((ts, n_cols), f32)

    return pl.pallas_call(
        _mixer_kernel,
        grid=grid,
        in_specs=[
            pl.BlockSpec((1, ts, d), lambda b, i: (b, i, 0)),
            mod_spec, mod_spec, mod_spec,
            pl.BlockSpec(w_in.shape, full2),
            pl.BlockSpec(w_out.shape, full2),
            pl.BlockSpec(cw.shape, full2),
            pl.BlockSpec(vec.shape, full2),
            pl.BlockSpec(mu.shape, full2),
            pl.BlockSpec(ln.shape, full2),
            pl.BlockSpec(wgate.shape, full2),
            pl.BlockSpec(wlora.shape, full2),
            pl.BlockSpec(bd.shape, full2),
        ],
        out_specs=pl.BlockSpec((1, ts, d), lambda b, i: (b, i, 0)),
        out_shape=jax.ShapeDtypeStruct((bsz, s, d), f32),
        scratch_shapes=[
            pltpu.VMEM((CONF_HIST + ts, D_GRP), f32),
            pltpu.VMEM((SMALL_HIST + ts, D_GRP), f32),
            pltpu.VMEM((SMALL_HIST + ts, D_GRP), f32),
            pltpu.VMEM((SUBLANES, C_COLS), f32),
            pltpu.VMEM((SUBLANES, D_GRP), f32),
            pltpu.VMEM((N_HEADS, HEAD_DIM, HEAD_DIM), f32),
            tile(D_GRP), tile(D_GRP), tile(D_GRP), tile(D_GRP), tile(D_GRP), tile(D_GRP),
            tile(D_GRP), tile(D_GRP), tile(D_GRP),
            tile(D_MODEL),
        ],
        compiler_params=pltpu.CompilerParams(
            dimension_semantics=("arbitrary", "arbitrary"),
            vmem_limit_bytes=VMEM_LIMIT_BYTES),
        name="token_mixers",
    )(x, sh, sc, g, w_in, w_out, cw, vec, mu, ln, wgate, wlora, bd)


def _first_index(mask, idx, sentinel):
    return jnp.min(jnp.where(mask, idx, sentinel), axis=0, keepdims=True)


def _router_kernel(x_ref, sh_ref, sc_ref, wr_ref, eb_ref, o_ref):
    x = x_ref[0]
    u = x * (1.0 + sc_ref[0]) + sh_ref[0]
    tm = u.shape[0]
    logits = _dot_f32(wr_ref[...], u, ((1,), (1,)))
    scores = jax.nn.sigmoid(logits)
    biased = scores + eb_ref[...]
    neg_inf = jnp.float32(-jnp.inf)
    sub = lax.broadcasted_iota(jnp.int32, (E_PER_GROUP, tm), 0).astype(jnp.float32)

    groups = [biased[g * E_PER_GROUP:(g + 1) * E_PER_GROUP, :] for g in range(N_GROUPS)]
    gscore = []
    for blk in groups:
        m1 = jnp.max(blk, axis=0, keepdims=True)
        first = _first_index(blk == m1, sub, float(E_PER_GROUP))
        m2 = jnp.max(jnp.where(sub == first, neg_inf, blk), axis=0, keepdims=True)
        gscore.append(m1 + m2)
    gs = jnp.concatenate(gscore, axis=0)
    gidx = lax.broadcasted_iota(jnp.int32, (N_GROUPS, tm), 0).astype(jnp.float32)
    keep = jnp.zeros((N_GROUPS, tm), jnp.float32)
    for _ in range(TOPK_GROUPS):
        m = jnp.max(gs, axis=0, keepdims=True)
        first = _first_index(gs == m, gidx, float(N_GROUPS))
        sel = gidx == first
        keep = jnp.where(sel, 1.0, keep)
        gs = jnp.where(sel, neg_inf, gs)

    masked = [jnp.where(keep[g:g + 1, :] > 0.5, groups[g], neg_inf) for g in range(N_GROUPS)]
    eidx = [sub + float(g * E_PER_GROUP) for g in range(N_GROUPS)]
    chosen = [jnp.zeros((E_PER_GROUP, tm), jnp.float32) for _ in range(N_GROUPS)]
    for _ in range(TOP_K):
        m = functools.reduce(jnp.maximum, [jnp.max(b, axis=0, keepdims=True) for b in masked])
        first = functools.reduce(
            jnp.minimum,
            [_first_index(masked[g] == m, eidx[g], float(N_EXPERTS)) for g in range(N_GROUPS)])
        for g in range(N_GROUPS):
            sel = eidx[g] == first
            chosen[g] = jnp.where(sel, 1.0, chosen[g])
            masked[g] = jnp.where(sel, neg_inf, masked[g])

    picked = [jnp.where(chosen[g] > 0.5, scores[g * E_PER_GROUP:(g + 1) * E_PER_GROUP, :], 0.0)
              for g in range(N_GROUPS)]
    total = functools.reduce(jnp.add, [jnp.sum(p, axis=0, keepdims=True) for p in picked])
    wts = jnp.concatenate([p / total * ROUTED_SCALE for p in picked]
                          + [jnp.zeros((LANES - N_EXPERTS, tm), jnp.float32)], axis=0)
    o_ref[...] = wts.T


def _router_layer(x, sh, sc, w_router_t, e_bias_col, tm):
    bsz, s, d = x.shape
    per_b = s // tm
    mod_spec = pl.BlockSpec((1, 1, d), lambda b, i: (b, 0, 0))
    return pl.pallas_call(
        _router_kernel,
        grid=(bsz, per_b),
        in_specs=[
            pl.BlockSpec((1, tm, d), lambda b, i: (b, i, 0)),
            mod_spec, mod_spec,
            pl.BlockSpec(w_router_t.shape, lambda b, i: (0, 0)),
            pl.BlockSpec(e_bias_col.shape, lambda b, i: (0, 0)),
        ],
        out_specs=pl.BlockSpec((tm, LANES), lambda b, i: (b * per_b + i, 0)),
        out_shape=jax.ShapeDtypeStruct((bsz * s, LANES), jnp.float32),
        compiler_params=pltpu.CompilerParams(
            dimension_semantics=("arbitrary", "arbitrary"),
            vmem_limit_bytes=VMEM_LIMIT_BYTES),
        name="moe_router",
    )(x, sh, sc, w_router_t, e_bias_col)


def _swiglu(u_bf16, w13, w2):
    hcat = jnp.dot(u_bf16, w13, preferred_element_type=jnp.float32)
    half = hcat.shape[1] // 2
    gate, up = hcat[:, :half], hcat[:, half:]
    act = gate * jax.nn.sigmoid(gate) * up
    return jnp.dot(_bf16(act), w2, preferred_element_type=jnp.float32)


def _expert_kernel(x_ref, sh_ref, sc_ref, g_ref, cwt_ref, w13_ref, w2_ref, ws13_ref, ws2_ref,
                   ln_ref, o_ref, u_s, acc):
    e = pl.program_id(2)

    @pl.when(e == 0)
    def _init():
        u = x_ref[0] * (1.0 + sc_ref[0]) + sh_ref[0]
        u_s[...] = _bf16(u)
        acc[...] = _swiglu(u_s[...], ws13_ref[...], ws2_ref[...])

    y = _swiglu(u_s[...], w13_ref[0], w2_ref[0])
    cwt = cwt_ref[...]
    lane = lax.broadcasted_iota(jnp.int32, cwt.shape, 1)
    col = jnp.sum(jnp.where(lane == e, cwt, 0.0), axis=1, keepdims=True)
    acc[...] += y * col

    @pl.when(e == pl.num_programs(2) - 1)
    def _finish():
        res = ln_ref[2:3, :] * x_ref[0] + (1.0 + g_ref[0]) * acc[...]
        o_ref[0] = _layer_norm(res, ln_ref[0:1, :], ln_ref[1:2, :])


def _expert_layer(x, sh, sc, g, cwt, w13, w2, ws13, ws2, ln, tm):
    bsz, s, d = x.shape
    per_b = s // tm
    n_exp = w13.shape[0]
    mod_spec = pl.BlockSpec((1, 1, d), lambda b, i, e: (b, 0, 0))
    full2 = lambda b, i, e: (0, 0)
    return pl.pallas_call(
        _expert_kernel,
        grid=(bsz, per_b, n_exp),
        in_specs=[
            pl.BlockSpec((1, tm, d), lambda b, i, e: (b, i, 0)),
            mod_spec, mod_spec, mod_spec,
            pl.BlockSpec((tm, LANES), lambda b, i, e: (b * per_b + i, 0)),
            pl.BlockSpec((1,) + w13.shape[1:], lambda b, i, e: (e, 0, 0)),
            pl.BlockSpec((1,) + w2.shape[1:], lambda b, i, e: (e, 0, 0)),
            pl.BlockSpec(ws13.shape, full2),
            pl.BlockSpec(ws2.shape, full2),
            pl.BlockSpec(ln.shape, full2),
        ],
        out_specs=pl.BlockSpec((1, tm, d), lambda b, i, e: (b, i, 0)),
        out_shape=jax.ShapeDtypeStruct((bsz, s, d), jnp.float32),
        scratch_shapes=[
            pltpu.VMEM((tm, d), jnp.bfloat16),
            pltpu.VMEM((tm, d), jnp.float32),
        ],
        compiler_params=pltpu.CompilerParams(
            dimension_semantics=("arbitrary", "arbitrary", "arbitrary"),
            vmem_limit_bytes=VMEM_LIMIT_BYTES),
        name="moe_experts",
    )(x, sh, sc, g, cwt, w13, w2, ws13, ws2, ln)


def _block_diag(w):
    h, n, _ = w.shape
    eye = jnp.eye(h, dtype=w.dtype)
    return (eye[:, None, :, None] * w[:, :, None, :]).reshape(h * n, h * n)


def _pad_rows(a, n_rows):
    return jnp.concatenate([a, jnp.zeros((n_rows - a.shape[0], a.shape[1]), a.dtype)], axis=0)


def _tile_rows(seq_len, want):
    t = min(want, seq_len)
    assert seq_len % t == 0 and t % RWKV_CHUNK == 0
    return t


def kernel(x, c, w_mod, b_mod, w_in, w_out, conv_a, conv_a_bias, ln_a_g, ln_a_b, conv_b, conv_b_bias, w_rg, b_rg, w_ig, b_ig, lru_lambda, mu_c, w0, w_w2, a0, w_a2, w_g2, k_k, k_a, r_k, gn_g, gn_b, conv_d, ln1_g, ln1_b, w_router, e_bias, w13, w2, ws13, ws2, ln2_g, ln2_b):
    n_layers = w_mod.shape[0]
    bsz, s, d = x.shape
    alpha_dn = (2.0 * n_layers) ** 0.25
    ts_mix = _tile_rows(s, 512)
    tm_moe = _tile_rows(s, 1024)
    bf = jnp.bfloat16

    mod = _modulation(c, w_mod, b_mod)
    bd = _block_diag(jnp.ones((N_HEADS, HEAD_DIM, HEAD_DIM), bf))
    alpha_row = jnp.full((1, d), alpha_dn, jnp.float32)

    for l in range(n_layers):
        sh1, sc1, g1, sh2, sc2, g2 = [mod[l, :, i * d:(i + 1) * d].reshape(bsz, 1, d)
                                      for i in range(6)]
        cw = _pad_rows(jnp.concatenate([conv_a[l], conv_b[l], conv_d[l]], axis=0), N_CW_ROWS)
        vec = _pad_rows(jnp.stack([
            conv_a_bias[l], ln_a_g[l], ln_a_b[l], conv_b_bias[l], b_rg[l], b_ig[l], lru_lambda[l],
            w0[l], a0[l], k_k[l], k_a[l], r_k[l].reshape(D_GRP), gn_g[l], gn_b[l]], axis=0),
            N_VEC_ROWS)
        wgate = jnp.concatenate([_block_diag(w_rg[l]), _block_diag(w_ig[l])], axis=1).astype(bf)
        wlora = jnp.zeros((LORA_W + LORA_A + LORA_G, 3 * D_GRP), jnp.float32)
        wlora = wlora.at[0:LORA_W, 0:D_GRP].set(w_w2[l])
        wlora = wlora.at[LORA_W:LORA_W + LORA_A, D_GRP:2 * D_GRP].set(w_a2[l])
        wlora = wlora.at[LORA_W + LORA_A:, 2 * D_GRP:].set(w_g2[l]).astype(bf)
        ln1 = jnp.concatenate([ln1_g[l][None], ln1_b[l][None], alpha_row], axis=0)
        ln2 = jnp.concatenate([ln2_g[l][None], ln2_b[l][None], alpha_row], axis=0)

        x = _token_mixer_layer(x, sh1, sc1, g1, w_in[l].astype(bf), w_out[l].astype(bf), cw, vec,
                               mu_c[l][None], ln1, wgate, wlora, bd, ts_mix)
        cwt = _router_layer(x, sh2, sc2, w_router[l].T, e_bias[l][:, None], tm_moe)
        x = _expert_layer(x, sh2, sc2, g2, cwt, w13[l].astype(bf), w2[l].astype(bf),
                          ws13[l].astype(bf), ws2[l].astype(bf), ln2, tm_moe)
    return x
```

```python
import functools

import jax
import jax.numpy as jnp
from jax import lax
from jax.experimental import pallas as pl
from jax.experimental.pallas import tpu as pltpu

D_MODEL = 1024
HEAD_DIM = 64
D_GRP = 256
N_HEADS = D_GRP // HEAD_DIM
CONF_KERNEL = 31
LRU_CONV = 4
LRU_C = 8.0
SHORT_CONV = 3
LORA_W, LORA_A, LORA_G = 32, 32, 64
C_COLS = 3 * D_GRP + LORA_W + LORA_A + LORA_G
P_IN = 4 * D_GRP + C_COLS + 3 * D_GRP
RWKV_GN_EPS = 64e-5
N_EXPERTS = 64
TOP_K = 8
N_GROUPS = 8
TOPK_GROUPS = 4
E_PER_GROUP = N_EXPERTS // N_GROUPS
D_EXPERT = 256
ROUTED_SCALE = 2.5
LN_EPS = 1e-5

OFF_A = 0
OFF_B = 2 * D_GRP
OFF_C = 4 * D_GRP
OFF_D = OFF_C + C_COLS

SUBLANES = 8
LANES = 128
VMEM_LIMIT_BYTES = 56 * 1024 * 1024

RWKV_CHUNK = 64
CONF_HIST = 32
SMALL_HIST = 8

(V_CONV_A_BIAS, V_LN_A_G, V_LN_A_B, V_CONV_B_BIAS, V_B_RG, V_B_IG, V_LRU_LAMBDA, V_W0, V_A0,
 V_K_K, V_K_A, V_R_K, V_GN_G, V_GN_B) = range(14)
N_VEC_ROWS = 16
CW_A, CW_B, CW_D = 0, CONF_KERNEL, CONF_KERNEL + LRU_CONV
N_CW_ROWS = 40


def _bf16(x):
    return x.astype(jnp.bfloat16)


def _dot(a, b, dims=((1,), (0,))):
    return lax.dot_general(_bf16(a), _bf16(b), (dims, ((), ())),
                           preferred_element_type=jnp.float32)


def _split(a):
    hi = _bf16(a)
    lo = _bf16(a - hi.astype(jnp.float32))
    return hi, lo


def _dot_f32(a, b, dims=((1,), (0,))):
    a_hi, a_lo = _split(a)
    b_hi, b_lo = _split(b)
    dn = (dims, ((), ()))
    f = functools.partial(lax.dot_general, dimension_numbers=dn,
                          preferred_element_type=jnp.float32)
    return f(a_hi, b_hi) + (f(a_hi, b_lo) + f(a_lo, b_hi))


def _dot_lhs_f32(a, b_bf16):
    a_hi, a_lo = _split(a)
    f = functools.partial(jnp.dot, preferred_element_type=jnp.float32)
    return f(a_hi, b_bf16) + f(a_lo, b_bf16)


def _layer_norm(x, g, b):
    mu = jnp.mean(x, axis=-1, keepdims=True)
    xc = x - mu
    var = jnp.mean(xc * xc, axis=-1, keepdims=True)
    return xc * lax.rsqrt(var + LN_EPS) * g + b


def _softplus(x):
    return jnp.maximum(x, 0.0) + jnp.log1p(jnp.exp(-jnp.abs(x)))


def _row_iota(shape):
    return lax.broadcasted_iota(jnp.int32, shape, 0)


def _shift_rows(x, d, fill):
    rolled = pltpu.roll(x, d, axis=0)
    return jnp.where(_row_iota(x.shape) >= d, rolled, fill)


def _mod_kernel(c_ref, w_ref, b_ref, o_ref):
    c = c_ref[...]
    c_act = c * jax.nn.sigmoid(c)
    o_ref[0] = _dot_f32(c_act, w_ref[0]) + b_ref[0]


def _modulation(c, w_mod, b_mod):
    n_layers, d, d6 = w_mod.shape
    bsz = c.shape[0]
    tn = D_MODEL
    return pl.pallas_call(
        _mod_kernel,
        grid=(n_layers, d6 // tn),
        in_specs=[
            pl.BlockSpec((bsz, d), lambda l, j: (0, 0)),
            pl.BlockSpec((1, d, tn), lambda l, j: (l, 0, j)),
            pl.BlockSpec((1, 1, tn), lambda l, j: (l, 0, j)),
        ],
        out_specs=pl.BlockSpec((1, bsz, tn), lambda l, j: (l, 0, j)),
        out_shape=jax.ShapeDtypeStruct((n_layers, bsz, d6), jnp.float32),
        compiler_params=pltpu.CompilerParams(
            dimension_semantics=("arbitrary", "arbitrary"),
            vmem_limit_bytes=VMEM_LIMIT_BYTES),
        name="adaln_modulation",
    )(c, w_mod, b_mod.reshape(n_layers, 1, d6))


def _causal_conv(buf_ref, hist, ts, w_rows, n_taps):
    acc = None
    for k in range(n_taps):
        off = hist - (n_taps - 1) + k
        term = w_rows[k:k + 1, :] * buf_ref[pl.ds(off, ts), :]
        acc = term if acc is None else acc + term
    return acc


def _head_sum(x, bd_ref):
    return _dot_lhs_f32(x, bd_ref[...])


_NN = ((2,), (1,))
_NT = ((2,), (2,))
_TN = ((1,), (1,))


def _bdot(a, b, dims):
    return lax.dot_general(_bf16(a), _bf16(b), (dims, ((0,), (0,))),
                           preferred_element_type=jnp.float32)


def _to_problems(x):
    n_chunks = x.shape[0] // RWKV_CHUNK
    return jnp.stack([x[c * RWKV_CHUNK:(c + 1) * RWKV_CHUNK, h * HEAD_DIM:(h + 1) * HEAD_DIM]
                      for c in range(n_chunks) for h in range(N_HEADS)], axis=0)


def _rwkv_chunked(kk, r, alpha, kf, v, lw, state):
    ts = kk.shape[0]
    n_chunks = ts // RWKV_CHUNK
    cl = lw
    seg_row = _row_iota(cl.shape) % RWKV_CHUNK
    d = 1
    while d < RWKV_CHUNK:
        cl = cl + jnp.where(seg_row >= d, pltpu.roll(cl, d, axis=0), 0.0)
        d *= 2
    cl_end = cl.reshape(n_chunks, RWKV_CHUNK, D_GRP)[:, RWKV_CHUNK - 1:RWKV_CHUNK, :]
    cl_end_rows = jnp.broadcast_to(cl_end, (n_chunks, RWKV_CHUNK, D_GRP)).reshape(ts, D_GRP)
    g_inv = jnp.exp(-cl)
    tail = jnp.exp(cl_end_rows - cl)
    g_end = jnp.exp(cl_end)
    bt = _to_problems(kk * jnp.exp(cl - lw))
    rt = _to_problems(r * jnp.exp(cl))
    at = _to_problems(alpha * g_inv)
    kt = _to_problems(kf * g_inv)
    ab = _to_problems(alpha * tail)
    kb = _to_problems(kf * tail)
    vp = _to_problems(v)

    c_sz = RWKV_CHUNK
    tri_r = lax.broadcasted_iota(jnp.int32, (1, c_sz, c_sz), 1)
    tri_c = lax.broadcasted_iota(jnp.int32, (1, c_sz, c_sz), 2)
    strict = tri_r > tri_c
    incl = tri_r >= tri_c
    eye = (tri_r == tri_c).astype(jnp.float32)

    amat = _bdot(jnp.concatenate([bt, rt], axis=1), jnp.concatenate([at, kt], axis=1), _NT)
    a_ba = jnp.where(strict, amat[:, :c_sz, :c_sz], 0.0)
    a_bk = jnp.where(strict, amat[:, :c_sz, c_sz:], 0.0)
    a_ra = jnp.where(incl, amat[:, c_sz:, :c_sz], 0.0)
    a_rk = jnp.where(incl, amat[:, c_sz:, c_sz:], 0.0)
    tinv = eye + a_ba
    pw = a_ba
    step = 2
    while step < c_sz:
        pw = _bdot(pw, pw, _NN)
        tinv = tinv + _bdot(pw, tinv, _NN)
        step *= 2
    tx = _bdot(tinv, jnp.concatenate([bt, _bdot(a_bk, vp, _NN)], axis=2), _NN)
    w_t, u_t = tx[:, :, :HEAD_DIM], tx[:, :, HEAD_DIM:]
    y_k = _bdot(a_rk, vp, _NN)

    s_cur = state[...]
    y_rows = []
    for c in range(n_chunks):
        p = slice(c * N_HEADS, (c + 1) * N_HEADS)
        res = _bdot(jnp.concatenate([w_t[p], rt[p]], axis=1), s_cur, _NT)
        u_c = res[:, :c_sz, :] + u_t[p]
        y_c = res[:, c_sz:, :] + _bdot(a_ra[p], u_c, _NN) + y_k[p]
        g_c = jnp.stack([g_end[c, :, h * HEAD_DIM:(h + 1) * HEAD_DIM] for h in range(N_HEADS)],
                        axis=0)
        s_cur = s_cur * g_c + _bdot(jnp.concatenate([u_c, vp[p]], axis=1),
                                    jnp.concatenate([ab[p], kb[p]], axis=1), _TN)
        y_rows.append(jnp.concatenate([y_c[h] for h in range(N_HEADS)], axis=1))
    state[...] = s_cur
    return jnp.concatenate(y_rows, axis=0)


def _mixer_kernel(x_ref, sh_ref, sc_ref, g_ref, w_in_ref, w_out_ref, cw_ref, vec_ref, mu_ref,
                  ln_ref, wgate_ref, wlora_ref, bd_ref, o_ref,
                  hbuf, xbbuf, zbuf, pc_last, lru_h, state, ymix):
    ts = x_ref.shape[1]
    s_idx = pl.program_id(1)

    @pl.when(s_idx == 0)
    def _reset():
        hbuf[pl.ds(0, CONF_HIST), :] = jnp.zeros((CONF_HIST, D_GRP), jnp.float32)
        xbbuf[pl.ds(0, SMALL_HIST), :] = jnp.zeros((SMALL_HIST, D_GRP), jnp.float32)
        zbuf[pl.ds(0, SMALL_HIST), :] = jnp.zeros((SMALL_HIST, D_GRP), jnp.float32)
        pc_last[...] = jnp.zeros(pc_last.shape, jnp.float32)
        lru_h[...] = jnp.zeros(lru_h.shape, jnp.float32)
        state[...] = jnp.zeros(state.shape, jnp.float32)

    x = x_ref[0]
    u = x * (1.0 + sc_ref[0]) + sh_ref[0]
    proj = jnp.dot(_bf16(u), w_in_ref[...], preferred_element_type=jnp.float32)

    vec = vec_ref[...]

    def vrow(i):
        return vec[i:i + 1, :]

    cw = cw_ref[...]

    val = proj[:, OFF_A:OFF_A + D_GRP]
    gate = proj[:, OFF_A + D_GRP:OFF_A + 2 * D_GRP]
    hbuf[pl.ds(CONF_HIST, ts), :] = val * jax.nn.sigmoid(gate)
    conv = _causal_conv(hbuf, CONF_HIST, ts, cw[CW_A:CW_A + CONF_KERNEL], CONF_KERNEL)
    conv = conv + vrow(V_CONV_A_BIAS)
    hbuf[pl.ds(0, CONF_HIST), :] = hbuf[pl.ds(ts, CONF_HIST), :]
    ln_a = _layer_norm(conv, vrow(V_LN_A_G), vrow(V_LN_A_B))
    ymix[:, 0:D_GRP] = ln_a * jax.nn.sigmoid(ln_a)

    xbbuf[pl.ds(SMALL_HIST, ts), :] = proj[:, OFF_B:OFF_B + D_GRP]
    gb = proj[:, OFF_B + D_GRP:OFF_B + 2 * D_GRP]
    ub = _causal_conv(xbbuf, SMALL_HIST, ts, cw[CW_B:CW_B + LRU_CONV], LRU_CONV)
    ub = ub + vrow(V_CONV_B_BIAS)
    xbbuf[pl.ds(0, SMALL_HIST), :] = xbbuf[pl.ds(ts, SMALL_HIST), :]
    gates = jnp.dot(_bf16(ub), wgate_ref[...], preferred_element_type=jnp.float32)
    r_gate = jax.nn.sigmoid(gates[:, :D_GRP] + vrow(V_B_RG))
    i_gate = jax.nn.sigmoid(gates[:, D_GRP:] + vrow(V_B_IG))
    log_a = (-LRU_C) * r_gate * _softplus(-vrow(V_LRU_LAMBDA))
    a_sc = jnp.exp(log_a)
    b_sc = jnp.sqrt(-jnp.tanh(log_a) * (a_sc * a_sc + 1.0)) * (i_gate * ub)
    d = 1
    while d < ts:
        a_sh = _shift_rows(a_sc, d, 1.0)
        b_sh = _shift_rows(b_sc, d, 0.0)
        b_sc = a_sc * b_sh + b_sc
        a_sc = a_sc * a_sh
        d *= 2
    h = b_sc + a_sc * lru_h[0:1, :]
    lru_h[0:1, :] = h[ts - 1:ts, :]
    ymix[:, D_GRP:2 * D_GRP] = h * jax.nn.gelu(gb, approximate=True)

    gbd = proj[:, OFF_D:OFF_D + D_GRP]
    zbuf[pl.ds(SMALL_HIST, ts), :] = (proj[:, OFF_D + D_GRP:OFF_D + 2 * D_GRP]
                                      * proj[:, OFF_D + 2 * D_GRP:OFF_D + 3 * D_GRP])
    convd = _causal_conv(zbuf, SMALL_HIST, ts, cw[CW_D:CW_D + SHORT_CONV], SHORT_CONV)
    zbuf[pl.ds(0, SMALL_HIST), :] = zbuf[pl.ds(ts, SMALL_HIST), :]
    ymix[:, 3 * D_GRP:4 * D_GRP] = gbd * convd

    pc = proj[:, OFF_C:OFF_C + C_COLS]
    prev = jnp.where(_row_iota(pc.shape) == 0, pc_last[0:1, :], pltpu.roll(pc, 1, axis=0))
    pc_last[0:1, :] = pc[ts - 1:ts, :]
    xs = pc + (prev - pc) * mu_ref[...]
    r = xs[:, 0:D_GRP]
    k = xs[:, D_GRP:2 * D_GRP]
    v = xs[:, 2 * D_GRP:3 * D_GRP]
    z = xs[:, 3 * D_GRP:C_COLS]
    lane = lax.broadcasted_iota(jnp.int32, z.shape, 1)
    z_act = jnp.where(lane < LORA_W, jnp.tanh(z),
                      jnp.where(lane < LORA_W + LORA_A, z, jax.nn.sigmoid(z)))
    lora = jnp.dot(_bf16(z_act), wlora_ref[...], preferred_element_type=jnp.float32)
    wlog = -_softplus(-(vrow(V_W0) + lora[:, 0:D_GRP])) - 0.5
    lw = -jnp.exp(wlog)
    a_icl = jax.nn.sigmoid(vrow(V_A0) + lora[:, D_GRP:2 * D_GRP])
    g_out = lora[:, 2 * D_GRP:3 * D_GRP]

    kk = k * vrow(V_K_K)
    kk = kk * lax.rsqrt(_head_sum(kk * kk, bd_ref) + 1e-12)
    kf = k * (1.0 + (a_icl - 1.0) * vrow(V_K_A))

    y = _rwkv_chunked(kk, r, -(kk * a_icl), kf, v, lw, state)
    inv_n = 1.0 / HEAD_DIM
    mu_y = _head_sum(y, bd_ref) * inv_n
    yc = y - mu_y
    var_y = _head_sum(yc * yc, bd_ref) * inv_n
    y = yc * lax.rsqrt(var_y + RWKV_GN_EPS) * vrow(V_GN_G) + vrow(V_GN_B)
    bonus = _head_sum(r * kf * vrow(V_R_K), bd_ref) * v
    ymix[:, 2 * D_GRP:3 * D_GRP] = (y + bonus) * g_out

    mix = jnp.dot(_bf16(ymix[...]), w_out_ref[...], preferred_element_type=jnp.float32)
    alpha_dn = ln_ref[2:3, :]
    res = alpha_dn * x + (1.0 + g_ref[0]) * mix
    o_ref[0] = _layer_norm(res, ln_ref[0:1, :], ln_ref[1:2, :])


def _token_mixer_layer(x, sh, sc, g, w_in, w_out, cw, vec, mu, ln, wgate, wlora, bd, ts):
    bsz, s, d = x.shape
    grid = (bsz, s // ts)
    full2 = lambda b, i: (0, 0)
    mod_spec = pl.BlockSpec((1, 1, d), lambda b, i: (b, 0, 0))
    f32 = jnp.float32

    def tile(n_cols):
        return pltpu.VMEM((ts, n_cols), f32)

    return pl.pallas_call(
        _mixer_kernel,
        grid=grid,
        in_specs=[
            pl.BlockSpec((1, ts, d), lambda b, i: (b, i, 0)),
            mod_spec, mod_spec, mod_spec,
            pl.BlockSpec(w_in.shape, full2),
            pl.BlockSpec(w_out.shape, full2),
            pl.BlockSpec(cw.shape, full2),
            pl.BlockSpec(vec.shape, full2),
            pl.BlockSpec(mu.shape, full2),
            pl.BlockSpec(ln.shape, full2),
            pl.BlockSpec(wgate.shape, full2),
            pl.BlockSpec(wlora.shape, full2),
            pl.BlockSpec(bd.shape, full2),
        ],
        out_specs=pl.BlockSpec((1, ts, d), lambda b, i: (b, i, 0)),
        out_shape=jax.ShapeDtypeStruct((bsz, s, d), f32),
        scratch_shapes=[
            pltpu.VMEM((CONF_HIST + ts, D_GRP), f32),
            pltpu.VMEM((SMALL_HIST + ts, D_GRP), f32),
            pltpu.VMEM((SMALL_HIST + ts, D_GRP), f32),
            pltpu.VMEM((SUBLANES, C_COLS), f32),
            pltpu.VMEM((SUBLANES, D_GRP), f32),
            pltpu.VMEM((N_HEADS, HEAD_DIM, HEAD_DIM), f32),
            tile(D_MODEL),
        ],
        compiler_params=pltpu.CompilerParams(
            dimension_semantics=("arbitrary", "arbitrary"),
            vmem_limit_bytes=VMEM_LIMIT_BYTES),
        name="token_mixers",
    )(x, sh, sc, g, w_in, w_out, cw, vec, mu, ln, wgate, wlora, bd)


def _first_index(mask, idx, sentinel):
    return jnp.min(jnp.where(mask, idx, sentinel), axis=0, keepdims=True)


def _router_kernel(x_ref, sh_ref, sc_ref, wr_ref, eb_ref, o_ref):
    x = x_ref[0]
    u = x * (1.0 + sc_ref[0]) + sh_ref[0]
    tm = u.shape[0]
    logits = _dot_f32(wr_ref[...], u, ((1,), (1,)))
    scores = jax.nn.sigmoid(logits)
    biased = scores + eb_ref[...]
    neg_inf = jnp.float32(-jnp.inf)
    sub = lax.broadcasted_iota(jnp.int32, (E_PER_GROUP, tm), 0).astype(jnp.float32)

    groups = [biased[g * E_PER_GROUP:(g + 1) * E_PER_GROUP, :] for g in range(N_GROUPS)]
    gscore = []
    for blk in groups:
        m1 = jnp.max(blk, axis=0, keepdims=True)
        first = _first_index(blk == m1, sub, float(E_PER_GROUP))
        m2 = jnp.max(jnp.where(sub == first, neg_inf, blk), axis=0, keepdims=True)
        gscore.append(m1 + m2)
    gs = jnp.concatenate(gscore, axis=0)
    gidx = lax.broadcasted_iota(jnp.int32, (N_GROUPS, tm), 0).astype(jnp.float32)
    keep = jnp.zeros((N_GROUPS, tm), jnp.float32)
    for _ in range(TOPK_GROUPS):
        m = jnp.max(gs, axis=0, keepdims=True)
        first = _first_index(gs == m, gidx, float(N_GROUPS))
        sel = gidx == first
        keep = jnp.where(sel, 1.0, keep)
        gs = jnp.where(sel, neg_inf, gs)

    masked = [jnp.where(keep[g:g + 1, :] > 0.5, groups[g], neg_inf) for g in range(N_GROUPS)]
    eidx = [sub + float(g * E_PER_GROUP) for g in range(N_GROUPS)]
    chosen = [jnp.zeros((E_PER_GROUP, tm), jnp.float32) for _ in range(N_GROUPS)]
    for _ in range(TOP_K):
        m = functools.reduce(jnp.maximum, [jnp.max(b, axis=0, keepdims=True) for b in masked])
        first = functools.reduce(
            jnp.minimum,
            [_first_index(masked[g] == m, eidx[g], float(N_EXPERTS)) for g in range(N_GROUPS)])
        for g in range(N_GROUPS):
            sel = eidx[g] == first
            chosen[g] = jnp.where(sel, 1.0, chosen[g])
            masked[g] = jnp.where(sel, neg_inf, masked[g])

    picked = [jnp.where(chosen[g] > 0.5, scores[g * E_PER_GROUP:(g + 1) * E_PER_GROUP, :], 0.0)
              for g in range(N_GROUPS)]
    total = functools.reduce(jnp.add, [jnp.sum(p, axis=0, keepdims=True) for p in picked])
    wts = jnp.concatenate([p / total * ROUTED_SCALE for p in picked]
                          + [jnp.zeros((LANES - N_EXPERTS, tm), jnp.float32)], axis=0)
    o_ref[...] = wts.T


def _router_layer(x, sh, sc, w_router_t, e_bias_col, tm):
    bsz, s, d = x.shape
    per_b = s // tm
    mod_spec = pl.BlockSpec((1, 1, d), lambda b, i: (b, 0, 0))
    return pl.pallas_call(
        _router_kernel,
        grid=(bsz, per_b),
        in_specs=[
            pl.BlockSpec((1, tm, d), lambda b, i: (b, i, 0)),
            mod_spec, mod_spec,
            pl.BlockSpec(w_router_t.shape, lambda b, i: (0, 0)),
            pl.BlockSpec(e_bias_col.shape, lambda b, i: (0, 0)),
        ],
        out_specs=pl.BlockSpec((tm, LANES), lambda b, i: (b * per_b + i, 0)),
        out_shape=jax.ShapeDtypeStruct((bsz * s, LANES), jnp.float32),
        compiler_params=pltpu.CompilerParams(
            dimension_semantics=("arbitrary", "arbitrary"),
            vmem_limit_bytes=VMEM_LIMIT_BYTES),
        name="moe_router",
    )(x, sh, sc, w_router_t, e_bias_col)


def _swiglu(u_bf16, w13, w2):
    hcat = jnp.dot(u_bf16, w13, preferred_element_type=jnp.float32)
    half = hcat.shape[1] // 2
    gate, up = hcat[:, :half], hcat[:, half:]
    act = gate * jax.nn.sigmoid(gate) * up
    return jnp.dot(_bf16(act), w2, preferred_element_type=jnp.float32)


def _expert_kernel(x_ref, sh_ref, sc_ref, g_ref, cwt_ref, w13_ref, w2_ref, ws13_ref, ws2_ref,
                   ln_ref, o_ref, u_s, acc):
    e = pl.program_id(2)

    @pl.when(e == 0)
    def _init():
        u = x_ref[0] * (1.0 + sc_ref[0]) + sh_ref[0]
        u_s[...] = _bf16(u)
        acc[...] = _swiglu(u_s[...], ws13_ref[...], ws2_ref[...])

    y = _swiglu(u_s[...], w13_ref[0], w2_ref[0])
    cwt = cwt_ref[...]
    lane = lax.broadcasted_iota(jnp.int32, cwt.shape, 1)
    col = jnp.sum(jnp.where(lane == e, cwt, 0.0), axis=1, keepdims=True)
    acc[...] += y * col

    @pl.when(e == pl.num_programs(2) - 1)
    def _finish():
        res = ln_ref[2:3, :] * x_ref[0] + (1.0 + g_ref[0]) * acc[...]
        o_ref[0] = _layer_norm(res, ln_ref[0:1, :], ln_ref[1:2, :])


def _expert_layer(x, sh, sc, g, cwt, w13, w2, ws13, ws2, ln, tm):
    bsz, s, d = x.shape
    per_b = s // tm
    n_exp = w13.shape[0]
    mod_spec = pl.BlockSpec((1, 1, d), lambda b, i, e: (b, 0, 0))
    full2 = lambda b, i, e: (0, 0)
    return pl.pallas_call(
        _expert_kernel,
        grid=(bsz, per_b, n_exp),
        in_specs=[
            pl.BlockSpec((1, tm, d), lambda b, i, e: (b, i, 0)),
            mod_spec, mod_spec, mod_spec,
            pl.BlockSpec((tm, LANES), lambda b, i, e: (b * per_b + i, 0)),
            pl.BlockSpec((1,) + w13.shape[1:], lambda b, i, e: (e, 0, 0)),
            pl.BlockSpec((1,) + w2.shape[1:], lambda b, i, e: (e, 0, 0)),
            pl.BlockSpec(ws13.shape, full2),
            pl.BlockSpec(ws2.shape, full2),
            pl.BlockSpec(ln.shape, full2),
        ],
        out_specs=pl.BlockSpec((1, tm, d), lambda b, i, e: (b, i, 0)),
        out_shape=jax.ShapeDtypeStruct((bsz, s, d), jnp.float32),
        scratch_shapes=[
            pltpu.VMEM((tm, d), jnp.bfloat16),
            pltpu.VMEM((tm, d), jnp.float32),
        ],
        compiler_params=pltpu.CompilerParams(
            dimension_semantics=("arbitrary", "arbitrary", "arbitrary"),
            vmem_limit_bytes=VMEM_LIMIT_BYTES),
        name="moe_experts",
    )(x, sh, sc, g, cwt, w13, w2, ws13, ws2, ln)


def _block_diag(w):
    h, n, _ = w.shape
    eye = jnp.eye(h, dtype=w.dtype)
    return (eye[:, None, :, None] * w[:, :, None, :]).reshape(h * n, h * n)


def _pad_rows(a, n_rows):
    return jnp.concatenate([a, jnp.zeros((n_rows - a.shape[0], a.shape[1]), a.dtype)], axis=0)


def _tile_rows(seq_len, want):
    t = min(want, seq_len)
    assert seq_len % t == 0 and t % RWKV_CHUNK == 0
    return t


def kernel(x, c, w_mod, b_mod, w_in, w_out, conv_a, conv_a_bias, ln_a_g, ln_a_b, conv_b, conv_b_bias, w_rg, b_rg, w_ig, b_ig, lru_lambda, mu_c, w0, w_w2, a0, w_a2, w_g2, k_k, k_a, r_k, gn_g, gn_b, conv_d, ln1_g, ln1_b, w_router, e_bias, w13, w2, ws13, ws2, ln2_g, ln2_b):
    n_layers = w_mod.shape[0]
    bsz, s, d = x.shape
    alpha_dn = (2.0 * n_layers) ** 0.25
    ts_mix = _tile_rows(s, 512)
    tm_moe = _tile_rows(s, 1024)
    bf = jnp.bfloat16

    mod = _modulation(c, w_mod, b_mod)
    bd = _block_diag(jnp.ones((N_HEADS, HEAD_DIM, HEAD_DIM), bf))
    alpha_row = jnp.full((1, d), alpha_dn, jnp.float32)

    for l in range(n_layers):
        sh1, sc1, g1, sh2, sc2, g2 = [mod[l, :, i * d:(i + 1) * d].reshape(bsz, 1, d)
                                      for i in range(6)]
        cw = _pad_rows(jnp.concatenate([conv_a[l], conv_b[l], conv_d[l]], axis=0), N_CW_ROWS)
        vec = _pad_rows(jnp.stack([
            conv_a_bias[l], ln_a_g[l], ln_a_b[l], conv_b_bias[l], b_rg[l], b_ig[l], lru_lambda[l],
            w0[l], a0[l], k_k[l], k_a[l], r_k[l].reshape(D_GRP), gn_g[l], gn_b[l]], axis=0),
            N_VEC_ROWS)
        wgate = jnp.concatenate([_block_diag(w_rg[l]), _block_diag(w_ig[l])], axis=1).astype(bf)
        wlora = jnp.zeros((LORA_W + LORA_A + LORA_G, 3 * D_GRP), jnp.float32)
        wlora = wlora.at[0:LORA_W, 0:D_GRP].set(w_w2[l])
        wlora = wlora.at[LORA_W:LORA_W + LORA_A, D_GRP:2 * D_GRP].set(w_a2[l])
        wlora = wlora.at[LORA_W + LORA_A:, 2 * D_GRP:].set(w_g2[l]).astype(bf)
        ln1 = jnp.concatenate([ln1_g[l][None], ln1_b[l][None], alpha_row], axis=0)
        ln2 = jnp.concatenate([ln2_g[l][None], ln2_b[l][None], alpha_row], axis=0)

        x = _token_mixer_layer(x, sh1, sc1, g1, w_in[l].astype(bf), w_out[l].astype(bf), cw, vec,
                               mu_c[l][None], ln1, wgate, wlora, bd, ts_mix)
        cwt = _router_layer(x, sh2, sc2, w_router[l].T, e_bias[l][:, None], tm_moe)
        x = _expert_layer(x, sh2, sc2, g2, cwt, w13[l].astype(bf), w2[l].astype(bf),
                          ws13[l].astype(bf), ws2[l].astype(bf), ln2, tm_moe)
    return x
```

```python
import functools

import jax
import jax.numpy as jnp
from jax import lax
from jax.experimental import pallas as pl
from jax.experimental.pallas import tpu as pltpu

D_MODEL = 1024
HEAD_DIM = 64
D_GRP = 256
N_HEADS = D_GRP // HEAD_DIM
CONF_KERNEL = 31
LRU_CONV = 4
LRU_C = 8.0
SHORT_CONV = 3
LORA_W, LORA_A, LORA_G = 32, 32, 64
C_COLS = 3 * D_GRP + LORA_W + LORA_A + LORA_G
P_IN = 4 * D_GRP + C_COLS + 3 * D_GRP
RWKV_GN_EPS = 64e-5
N_EXPERTS = 64
TOP_K = 8
N_GROUPS = 8
TOPK_GROUPS = 4
E_PER_GROUP = N_EXPERTS // N_GROUPS
D_EXPERT = 256
ROUTED_SCALE = 2.5
LN_EPS = 1e-5

OFF_A = 0
OFF_B = 2 * D_GRP
OFF_C = 4 * D_GRP
OFF_D = OFF_C + C_COLS

SUBLANES = 8
LANES = 128
VMEM_LIMIT_BYTES = 56 * 1024 * 1024

SLAB_ROWS = D_MODEL // LANES
EXPERT_ROWS = 144
W_SLOTS = 3
W_LOOKAHEAD = W_SLOTS - 1
RWKV_CHUNK = 64
CONF_HIST = 32
SMALL_HIST = 8

(V_CONV_A_BIAS, V_LN_A_G, V_LN_A_B, V_CONV_B_BIAS, V_B_RG, V_B_IG, V_LRU_LAMBDA, V_W0, V_A0,
 V_K_K, V_K_A, V_R_K, V_GN_G, V_GN_B) = range(14)
N_VEC_ROWS = 16
CW_A, CW_B, CW_D = 0, CONF_KERNEL, CONF_KERNEL + LRU_CONV
N_CW_ROWS = 40


def _bf16(x):
    return x.astype(jnp.bfloat16)


def _dot(a, b, dims=((1,), (0,))):
    return lax.dot_general(_bf16(a), _bf16(b), (dims, ((), ())),
                           preferred_element_type=jnp.float32)


def _split(a):
    hi = _bf16(a)
    lo = _bf16(a - hi.astype(jnp.float32))
    return hi, lo


def _dot_f32(a, b, dims=((1,), (0,))):
    a_hi, a_lo = _split(a)
    b_hi, b_lo = _split(b)
    dn = (dims, ((), ()))
    f = functools.partial(lax.dot_general, dimension_numbers=dn,
                          preferred_element_type=jnp.float32)
    return f(a_hi, b_hi) + (f(a_hi, b_lo) + f(a_lo, b_hi))


def _dot_lhs_f32(a, b_bf16):
    a_hi, a_lo = _split(a)
    f = functools.partial(jnp.dot, preferred_element_type=jnp.float32)
    return f(a_hi, b_bf16) + f(a_lo, b_bf16)


def _layer_norm(x, g, b):
    mu = jnp.mean(x, axis=-1, keepdims=True)
    xc = x - mu
    var = jnp.mean(xc * xc, axis=-1, keepdims=True)
    return xc * lax.rsqrt(var + LN_EPS) * g + b


def _softplus(x):
    return jnp.maximum(x, 0.0) + jnp.log1p(jnp.exp(-jnp.abs(x)))


def _row_iota(shape):
    return lax.broadcasted_iota(jnp.int32, shape, 0)


def _shift_rows(x, d, fill):
    rolled = pltpu.roll(x, d, axis=0)
    return jnp.where(_row_iota(x.shape) >= d, rolled, fill)


def _mod_kernel(c_ref, w_ref, b_ref, o_ref):
    c = c_ref[...]
    c_act = c * jax.nn.sigmoid(c)
    o_ref[0] = _dot_f32(c_act, w_ref[0]) + b_ref[0]


def _modulation(c, w_mod, b_mod):
    n_layers, d, d6 = w_mod.shape
    bsz = c.shape[0]
    tn = D_MODEL
    return pl.pallas_call(
        _mod_kernel,
        grid=(n_layers, d6 // tn),
        in_specs=[
            pl.BlockSpec((bsz, d), lambda l, j: (0, 0)),
            pl.BlockSpec((1, d, tn), lambda l, j: (l, 0, j)),
            pl.BlockSpec((1, 1, tn), lambda l, j: (l, 0, j)),
        ],
        out_specs=pl.BlockSpec((1, bsz, tn), lambda l, j: (l, 0, j)),
        out_shape=jax.ShapeDtypeStruct((n_layers, bsz, d6), jnp.float32),
        compiler_params=pltpu.CompilerParams(
            dimension_semantics=("arbitrary", "arbitrary"),
            vmem_limit_bytes=VMEM_LIMIT_BYTES),
        name="adaln_modulation",
    )(c, w_mod, b_mod.reshape(n_layers, 1, d6))


def _causal_conv(buf_ref, hist, ts, w_rows, n_taps):
    acc = None
    for k in range(n_taps):
        off = hist - (n_taps - 1) + k
        term = w_rows[k:k + 1, :] * buf_ref[pl.ds(off, ts), :]
        acc = term if acc is None else acc + term
    return acc


def _head_sum(x, bd_ref):
    return _dot_lhs_f32(x, bd_ref[...])


_NN = ((2,), (1,))
_NT = ((2,), (2,))
_TN = ((1,), (1,))


def _bdot(a, b, dims):
    return lax.dot_general(_bf16(a), _bf16(b), (dims, ((0,), (0,))),
                           preferred_element_type=jnp.float32)


def _to_problems(x):
    n_chunks = x.shape[0] // RWKV_CHUNK
    return jnp.stack([x[c * RWKV_CHUNK:(c + 1) * RWKV_CHUNK, h * HEAD_DIM:(h + 1) * HEAD_DIM]
                      for c in range(n_chunks) for h in range(N_HEADS)], axis=0)


def _rwkv_chunked(kk, r, alpha, kf, v, lw, state):
    ts = kk.shape[0]
    n_chunks = ts // RWKV_CHUNK
    cl = lw
    seg_row = _row_iota(cl.shape) % RWKV_CHUNK
    d = 1
    while d < RWKV_CHUNK:
        cl = cl + jnp.where(seg_row >= d, pltpu.roll(cl, d, axis=0), 0.0)
        d *= 2
    cl_end = cl.reshape(n_chunks, RWKV_CHUNK, D_GRP)[:, RWKV_CHUNK - 1:RWKV_CHUNK, :]
    cl_end_rows = jnp.broadcast_to(cl_end, (n_chunks, RWKV_CHUNK, D_GRP)).reshape(ts, D_GRP)
    g_inv = jnp.exp(-cl)
    tail = jnp.exp(cl_end_rows - cl)
    g_end = jnp.exp(cl_end)
    bt = _to_problems(kk * jnp.exp(cl - lw))
    rt = _to_problems(r * jnp.exp(cl))
    at = _to_problems(alpha * g_inv)
    kt = _to_problems(kf * g_inv)
    ab = _to_problems(alpha * tail)
    kb = _to_problems(kf * tail)
    vp = _to_problems(v)

    c_sz = RWKV_CHUNK
    tri_r = lax.broadcasted_iota(jnp.int32, (1, c_sz, c_sz), 1)
    tri_c = lax.broadcasted_iota(jnp.int32, (1, c_sz, c_sz), 2)
    strict = tri_r > tri_c
    incl = tri_r >= tri_c
    eye = (tri_r == tri_c).astype(jnp.float32)

    amat = _bdot(jnp.concatenate([bt, rt], axis=1), jnp.concatenate([at, kt], axis=1), _NT)
    a_ba = jnp.where(strict, amat[:, :c_sz, :c_sz], 0.0)
    a_bk = jnp.where(strict, amat[:, :c_sz, c_sz:], 0.0)
    a_ra = jnp.where(incl, amat[:, c_sz:, :c_sz], 0.0)
    a_rk = jnp.where(incl, amat[:, c_sz:, c_sz:], 0.0)
    tinv = eye + a_ba
    pw = a_ba
    step = 2
    while step < c_sz:
        pw = _bdot(pw, pw, _NN)
        tinv = tinv + _bdot(pw, tinv, _NN)
        step *= 2
    tx = _bdot(tinv, jnp.concatenate([bt, _bdot(a_bk, vp, _NN)], axis=2), _NN)
    w_t, u_t = tx[:, :, :HEAD_DIM], tx[:, :, HEAD_DIM:]
    y_k = _bdot(a_rk, vp, _NN)

    s_cur = state[...]
    y_rows = []
    for c in range(n_chunks):
        p = slice(c * N_HEADS, (c + 1) * N_HEADS)
        res = _bdot(jnp.concatenate([w_t[p], rt[p]], axis=1), s_cur, _NT)
        u_c = res[:, :c_sz, :] + u_t[p]
        y_c = res[:, c_sz:, :] + _bdot(a_ra[p], u_c, _NN) + y_k[p]
        g_c = jnp.stack([g_end[c, :, h * HEAD_DIM:(h + 1) * HEAD_DIM] for h in range(N_HEADS)],
                        axis=0)
        s_cur = s_cur * g_c + _bdot(jnp.concatenate([u_c, vp[p]], axis=1),
                                    jnp.concatenate([ab[p], kb[p]], axis=1), _TN)
        y_rows.append(jnp.concatenate([y_c[h] for h in range(N_HEADS)], axis=1))
    state[...] = s_cur
    return jnp.concatenate(y_rows, axis=0)


def _mixer_kernel(x_ref, sh_ref, sc_ref, g_ref, w_in_ref, w_out_ref, cw_ref, vec_ref, mu_ref,
                  ln_ref, wgate_ref, wlora_ref, bd_ref, o_ref,
                  hbuf, xbbuf, zbuf, pc_last, lru_h, state, ymix):
    ts = x_ref.shape[1]
    s_idx = pl.program_id(1)

    @pl.when(s_idx == 0)
    def _reset():
        hbuf[pl.ds(0, CONF_HIST), :] = jnp.zeros((CONF_HIST, D_GRP), jnp.float32)
        xbbuf[pl.ds(0, SMALL_HIST), :] = jnp.zeros((SMALL_HIST, D_GRP), jnp.float32)
        zbuf[pl.ds(0, SMALL_HIST), :] = jnp.zeros((SMALL_HIST, D_GRP), jnp.float32)
        pc_last[...] = jnp.zeros(pc_last.shape, jnp.float32)
        lru_h[...] = jnp.zeros(lru_h.shape, jnp.float32)
        state[...] = jnp.zeros(state.shape, jnp.float32)

    x = x_ref[0]
    u = x * (1.0 + sc_ref[0]) + sh_ref[0]
    proj = jnp.dot(_bf16(u), w_in_ref[...], preferred_element_type=jnp.float32)

    vec = vec_ref[...]

    def vrow(i):
        return vec[i:i + 1, :]

    cw = cw_ref[...]

    val = proj[:, OFF_A:OFF_A + D_GRP]
    gate = proj[:, OFF_A + D_GRP:OFF_A + 2 * D_GRP]
    hbuf[pl.ds(CONF_HIST, ts), :] = val * jax.nn.sigmoid(gate)
    conv = _causal_conv(hbuf, CONF_HIST, ts, cw[CW_A:CW_A + CONF_KERNEL], CONF_KERNEL)
    conv = conv + vrow(V_CONV_A_BIAS)
    hbuf[pl.ds(0, CONF_HIST), :] = hbuf[pl.ds(ts, CONF_HIST), :]
    ln_a = _layer_norm(conv, vrow(V_LN_A_G), vrow(V_LN_A_B))
    ymix[:, 0:D_GRP] = ln_a * jax.nn.sigmoid(ln_a)

    xbbuf[pl.ds(SMALL_HIST, ts), :] = proj[:, OFF_B:OFF_B + D_GRP]
    gb = proj[:, OFF_B + D_GRP:OFF_B + 2 * D_GRP]
    ub = _causal_conv(xbbuf, SMALL_HIST, ts, cw[CW_B:CW_B + LRU_CONV], LRU_CONV)
    ub = ub + vrow(V_CONV_B_BIAS)
    xbbuf[pl.ds(0, SMALL_HIST), :] = xbbuf[pl.ds(ts, SMALL_HIST), :]
    gates = jnp.dot(_bf16(ub), wgate_ref[...], preferred_element_type=jnp.float32)
    r_gate = jax.nn.sigmoid(gates[:, :D_GRP] + vrow(V_B_RG))
    i_gate = jax.nn.sigmoid(gates[:, D_GRP:] + vrow(V_B_IG))
    log_a = (-LRU_C) * r_gate * _softplus(-vrow(V_LRU_LAMBDA))
    a_sc = jnp.exp(log_a)
    b_sc = jnp.sqrt(-jnp.tanh(log_a) * (a_sc * a_sc + 1.0)) * (i_gate * ub)
    d = 1
    while d < ts:
        a_sh = _shift_rows(a_sc, d, 1.0)
        b_sh = _shift_rows(b_sc, d, 0.0)
        b_sc = a_sc * b_sh + b_sc
        a_sc = a_sc * a_sh
        d *= 2
    h = b_sc + a_sc * lru_h[0:1, :]
    lru_h[0:1, :] = h[ts - 1:ts, :]
    ymix[:, D_GRP:2 * D_GRP] = h * jax.nn.gelu(gb, approximate=True)

    gbd = proj[:, OFF_D:OFF_D + D_GRP]
    zbuf[pl.ds(SMALL_HIST, ts), :] = (proj[:, OFF_D + D_GRP:OFF_D + 2 * D_GRP]
                                      * proj[:, OFF_D + 2 * D_GRP:OFF_D + 3 * D_GRP])
    convd = _causal_conv(zbuf, SMALL_HIST, ts, cw[CW_D:CW_D + SHORT_CONV], SHORT_CONV)
    zbuf[pl.ds(0, SMALL_HIST), :] = zbuf[pl.ds(ts, SMALL_HIST), :]
    ymix[:, 3 * D_GRP:4 * D_GRP] = gbd * convd

    pc = proj[:, OFF_C:OFF_C + C_COLS]
    prev = jnp.where(_row_iota(pc.shape) == 0, pc_last[0:1, :], pltpu.roll(pc, 1, axis=0))
    pc_last[0:1, :] = pc[ts - 1:ts, :]
    xs = pc + (prev - pc) * mu_ref[...]
    r = xs[:, 0:D_GRP]
    k = xs[:, D_GRP:2 * D_GRP]
    v = xs[:, 2 * D_GRP:3 * D_GRP]
    z = xs[:, 3 * D_GRP:C_COLS]
    lane = lax.broadcasted_iota(jnp.int32, z.shape, 1)
    z_act = jnp.where(lane < LORA_W, jnp.tanh(z),
                      jnp.where(lane < LORA_W + LORA_A, z, jax.nn.sigmoid(z)))
    lora = jnp.dot(_bf16(z_act), wlora_ref[...], preferred_element_type=jnp.float32)
    wlog = -_softplus(-(vrow(V_W0) + lora[:, 0:D_GRP])) - 0.5
    lw = -jnp.exp(wlog)
    a_icl = jax.nn.sigmoid(vrow(V_A0) + lora[:, D_GRP:2 * D_GRP])
    g_out = lora[:, 2 * D_GRP:3 * D_GRP]

    kk = k * vrow(V_K_K)
    kk = kk * lax.rsqrt(_head_sum(kk * kk, bd_ref) + 1e-12)
    kf = k * (1.0 + (a_icl - 1.0) * vrow(V_K_A))

    y = _rwkv_chunked(kk, r, -(kk * a_icl), kf, v, lw, state)
    inv_n = 1.0 / HEAD_DIM
    mu_y = _head_sum(y, bd_ref) * inv_n
    yc = y - mu_y
    var_y = _head_sum(yc * yc, bd_ref) * inv_n
    y = yc * lax.rsqrt(var_y + RWKV_GN_EPS) * vrow(V_GN_G) + vrow(V_GN_B)
    bonus = _head_sum(r * kf * vrow(V_R_K), bd_ref) * v
    ymix[:, 2 * D_GRP:3 * D_GRP] = (y + bonus) * g_out

    mix = jnp.dot(_bf16(ymix[...]), w_out_ref[...], preferred_element_type=jnp.float32)
    alpha_dn = ln_ref[2:3, :]
    res = alpha_dn * x + (1.0 + g_ref[0]) * mix
    o_ref[0] = _layer_norm(res, ln_ref[0:1, :], ln_ref[1:2, :])


def _token_mixer_layer(x, sh, sc, g, w_in, w_out, cw, vec, mu, ln, wgate, wlora, bd, ts):
    bsz, s, d = x.shape
    grid = (bsz, s // ts)
    full2 = lambda b, i: (0, 0)
    mod_spec = pl.BlockSpec((1, 1, d), lambda b, i: (b, 0, 0))
    f32 = jnp.float32

    def tile(n_cols):
        return pltpu.VMEM((ts, n_cols), f32)

    return pl.pallas_call(
        _mixer_kernel,
        grid=grid,
        in_specs=[
            pl.BlockSpec((1, ts, d), lambda b, i: (b, i, 0)),
            mod_spec, mod_spec, mod_spec,
            pl.BlockSpec(w_in.shape, full2),
            pl.BlockSpec(w_out.shape, full2),
            pl.BlockSpec(cw.shape, full2),
            pl.BlockSpec(vec.shape, full2),
            pl.BlockSpec(mu.shape, full2),
            pl.BlockSpec(ln.shape, full2),
            pl.BlockSpec(wgate.shape, full2),
            pl.BlockSpec(wlora.shape, full2),
            pl.BlockSpec(bd.shape, full2),
        ],
        out_specs=pl.BlockSpec((1, ts, d), lambda b, i: (b, i, 0)),
        out_shape=jax.ShapeDtypeStruct((bsz, s, d), f32),
        scratch_shapes=[
            pltpu.VMEM((CONF_HIST + ts, D_GRP), f32),
            pltpu.VMEM((SMALL_HIST + ts, D_GRP), f32),
            pltpu.VMEM((SMALL_HIST + ts, D_GRP), f32),
            pltpu.VMEM((SUBLANES, C_COLS), f32),
            pltpu.VMEM((SUBLANES, D_GRP), f32),
            pltpu.VMEM((N_HEADS, HEAD_DIM, HEAD_DIM), f32),
            tile(D_MODEL),
        ],
        compiler_params=pltpu.CompilerParams(
            dimension_semantics=("arbitrary", "arbitrary"),
            vmem_limit_bytes=VMEM_LIMIT_BYTES),
        name="token_mixers",
    )(x, sh, sc, g, w_in, w_out, cw, vec, mu, ln, wgate, wlora, bd)


def _first_index(mask, idx, sentinel):
    return jnp.min(jnp.where(mask, idx, sentinel), axis=0, keepdims=True)


def _router_kernel(x_ref, sh_ref, sc_ref, wr_ref, eb_ref, su_ref, sl_ref,
                   slot_ref, wt_ref, plan_ref):
    x = x_ref[0]
    u = x * (1.0 + sc_ref[0]) + sh_ref[0]
    tm = u.shape[0]
    logits = _dot_f32(wr_ref[...], u, ((1,), (1,)))
    scores = jax.nn.sigmoid(logits)
    biased = scores + eb_ref[...]
    neg_inf = jnp.float32(-jnp.inf)
    sub = lax.broadcasted_iota(jnp.int32, (E_PER_GROUP, tm), 0).astype(jnp.float32)

    groups = [biased[g * E_PER_GROUP:(g + 1) * E_PER_GROUP, :] for g in range(N_GROUPS)]
    gscore = []
    for blk in groups:
        m1 = jnp.max(blk, axis=0, keepdims=True)
        first = _first_index(blk == m1, sub, float(E_PER_GROUP))
        m2 = jnp.max(jnp.where(sub == first, neg_inf, blk), axis=0, keepdims=True)
        gscore.append(m1 + m2)
    gs = jnp.concatenate(gscore, axis=0)
    gidx = lax.broadcasted_iota(jnp.int32, (N_GROUPS, tm), 0).astype(jnp.float32)
    keep = jnp.zeros((N_GROUPS, tm), jnp.float32)
    for _ in range(TOPK_GROUPS):
        m = jnp.max(gs, axis=0, keepdims=True)
        first = _first_index(gs == m, gidx, float(N_GROUPS))
        sel = gidx == first
        keep = jnp.where(sel, 1.0, keep)
        gs = jnp.where(sel, neg_inf, gs)

    masked = [jnp.where(keep[g:g + 1, :] > 0.5, groups[g], neg_inf) for g in range(N_GROUPS)]
    eidx = [sub + float(g * E_PER_GROUP) for g in range(N_GROUPS)]
    chosen = [jnp.zeros((E_PER_GROUP, tm), jnp.float32) for _ in range(N_GROUPS)]
    for _ in range(TOP_K):
        m = functools.reduce(jnp.maximum, [jnp.max(b, axis=0, keepdims=True) for b in masked])
        first = functools.reduce(
            jnp.minimum,
            [_first_index(masked[g] == m, eidx[g], float(N_EXPERTS)) for g in range(N_GROUPS)])
        for g in range(N_GROUPS):
            sel = eidx[g] == first
            chosen[g] = jnp.where(sel, 1.0, chosen[g])
            masked[g] = jnp.where(sel, neg_inf, masked[g])

    picked = [jnp.where(chosen[g] > 0.5, scores[g * E_PER_GROUP:(g + 1) * E_PER_GROUP, :], 0.0)
              for g in range(N_GROUPS)]
    total = functools.reduce(jnp.add, [jnp.sum(p, axis=0, keepdims=True) for p in picked])
    wts = jnp.concatenate([p / total * ROUTED_SCALE for p in picked], axis=0)

    sel = jnp.concatenate(chosen, axis=0)
    sel_bf = _bf16(sel)
    rank = jnp.dot(sel_bf, su_ref[...], preferred_element_type=jnp.float32)
    cnt = jnp.sum(sel, axis=1, keepdims=True)
    off = _dot_f32(sl_ref[...].astype(jnp.float32), jnp.broadcast_to(cnt, (N_EXPERTS, LANES)))
    slot = off[:, 0:1] + rank
    kidx = jnp.dot(sl_ref[...], sel_bf, preferred_element_type=jnp.float32)
    slot_rows, wt_rows = [], []
    for kk in range(TOP_K):
        pick = (sel > 0.5) & (kidx == float(kk))
        slot_rows.append(jnp.sum(jnp.where(pick, slot, 0.0), axis=0, keepdims=True))
        wt_rows.append(jnp.sum(jnp.where(pick, wts, 0.0), axis=0, keepdims=True))
    slot_ref[0] = (jnp.concatenate(slot_rows, axis=0) * float(SLAB_ROWS)).astype(jnp.int32)
    wt_ref[0] = jnp.concatenate(wt_rows, axis=0)
    e_sub = lax.broadcasted_iota(jnp.int32, (N_EXPERTS, LANES), 0)
    e_lane = lax.broadcasted_iota(jnp.int32, (N_EXPERTS, LANES), 1)
    diag = e_sub == e_lane
    off_row = jnp.sum(jnp.where(diag, off, 0.0), axis=0, keepdims=True)
    cnt_row = jnp.sum(jnp.where(diag, jnp.broadcast_to(cnt, (N_EXPERTS, LANES)), 0.0),
                      axis=0, keepdims=True)
    plan_ref[0] = jnp.concatenate([off_row, cnt_row], axis=0).astype(jnp.int32)


def _router_layer(x, sh, sc, w_router_t, e_bias_col, su, sl, tm):
    bsz, s, d = x.shape
    per_b = s // tm
    n_tiles = bsz * per_b
    mod_spec = pl.BlockSpec((1, 1, d), lambda b, i: (b, 0, 0))
    const2 = lambda b, i: (0, 0)
    tile3 = lambda b, i: (b * per_b + i, 0, 0)
    return pl.pallas_call(
        _router_kernel,
        grid=(bsz, per_b),
        in_specs=[
            pl.BlockSpec((1, tm, d), lambda b, i: (b, i, 0)),
            mod_spec, mod_spec,
            pl.BlockSpec(w_router_t.shape, const2),
            pl.BlockSpec(e_bias_col.shape, const2),
            pl.BlockSpec(su.shape, const2),
            pl.BlockSpec(sl.shape, const2),
        ],
        out_specs=[
            pl.BlockSpec((1, TOP_K, tm), tile3),
            pl.BlockSpec((1, TOP_K, tm), tile3),
            pl.BlockSpec((1, 2, LANES), tile3),
        ],
        out_shape=[
            jax.ShapeDtypeStruct((n_tiles, TOP_K, tm), jnp.int32),
            jax.ShapeDtypeStruct((n_tiles, TOP_K, tm), jnp.float32),
            jax.ShapeDtypeStruct((n_tiles, 2, LANES), jnp.int32),
        ],
        compiler_params=pltpu.CompilerParams(
            dimension_semantics=("arbitrary", "arbitrary"),
            vmem_limit_bytes=VMEM_LIMIT_BYTES),
        name="moe_router",
    )(x, sh, sc, w_router_t, e_bias_col, su, sl)


def _swiglu(u_bf16, w13, w2):
    hcat = jnp.dot(u_bf16, w13, preferred_element_type=jnp.float32)
    half = hcat.shape[1] // 2
    gate, up = hcat[:, :half], hcat[:, half:]
    act = gate * jax.nn.sigmoid(gate) * up
    return jnp.dot(_bf16(act), w2, preferred_element_type=jnp.float32)


def _weight_copies(w13_hbm, w2_hbm, w13_buf, w2_buf, sems, e, slot):
    return (pltpu.make_async_copy(w13_hbm.at[e], w13_buf.at[slot], sems.at[0, slot]),
            pltpu.make_async_copy(w2_hbm.at[e], w2_buf.at[slot], sems.at[1, slot]))


def _expert_kernel(plan_ref, slot_ref, wt_ref, x_ref, sh_ref, sc_ref, g_ref, w13_hbm, w2_hbm,
                   ws13_ref, ws2_ref, ln_ref, o_ref, rows_f, xf, w13_buf, w2_buf, sems):
    i = pl.program_id(0)
    tm = x_ref.shape[0]
    n_exp = w13_hbm.shape[0]
    n_col = x_ref.shape[1] // LANES

    @pl.when(i == 0)
    def _clear():
        xf[...] = jnp.zeros(xf.shape, xf.dtype)

    for e0 in range(W_LOOKAHEAD):
        for cp in _weight_copies(w13_hbm, w2_hbm, w13_buf, w2_buf, sems, e0, e0):
            cp.start()

    u = x_ref[...] * (1.0 + sc_ref[0]) + sh_ref[0]
    for j in range(n_col):
        rows_f[pl.ds(j, tm, stride=SLAB_ROWS), :] = u[:, j * LANES:(j + 1) * LANES]

    def scatter(t, carry):
        row = rows_f[pl.ds(pl.multiple_of(t * SLAB_ROWS, SLAB_ROWS), SLAB_ROWS), :]
        for k in range(TOP_K):
            dst = pl.multiple_of(slot_ref[k * tm + t], SLAB_ROWS)
            xf[pl.ds(dst, SLAB_ROWS), :] = row
        return carry

    lax.fori_loop(0, tm, scatter, 0)

    def expert(e, carry):
        slot = lax.rem(e, W_SLOTS)
        for cp in _weight_copies(w13_hbm, w2_hbm, w13_buf, w2_buf, sems, e, slot):
            cp.wait()

        @pl.when(e + W_LOOKAHEAD < n_exp)
        def _prefetch():
            nxt = e + W_LOOKAHEAD
            for cp in _weight_copies(w13_hbm, w2_hbm, w13_buf, w2_buf, sems, nxt,
                                     lax.rem(nxt, W_SLOTS)):
                cp.start()

        first = plan_ref[i * 2 * LANES + e]
        count = plan_ref[i * 2 * LANES + LANES + e]
        n_blocks = lax.div(count + (EXPERT_ROWS - 1), EXPERT_ROWS)

        def block(j, c2):
            base = (first + j * EXPERT_ROWS) * SLAB_ROWS
            old = [xf[pl.ds(base + q, EXPERT_ROWS, stride=SLAB_ROWS), :] for q in range(n_col)]
            xb = jnp.concatenate([_bf16(o) for o in old], axis=1)
            y = _swiglu(xb, w13_buf[slot], w2_buf[slot])
            live = (lax.broadcasted_iota(jnp.int32, (EXPERT_ROWS, LANES), 0)
                    < count - j * EXPERT_ROWS)
            for q in range(n_col):
                xf[pl.ds(base + q, EXPERT_ROWS, stride=SLAB_ROWS), :] = jnp.where(
                    live, y[:, q * LANES:(q + 1) * LANES], old[q])
            return c2

        lax.fori_loop(0, n_blocks, block, 0)
        return carry

    lax.fori_loop(0, n_exp, expert, 0)

    def gather(t, carry):
        acc = jnp.zeros((SLAB_ROWS, LANES), jnp.float32)
        for k in range(TOP_K):
            src = pl.multiple_of(slot_ref[k * tm + t], SLAB_ROWS)
            acc = acc + wt_ref[k * tm + t] * xf[pl.ds(src, SLAB_ROWS), :]
        rows_f[pl.ds(pl.multiple_of(t * SLAB_ROWS, SLAB_ROWS), SLAB_ROWS), :] = acc
        return carry

    lax.fori_loop(0, tm, gather, 0)
    routed = jnp.concatenate([rows_f[pl.ds(j, tm, stride=SLAB_ROWS), :] for j in range(n_col)],
                             axis=1)
    x = x_ref[...]
    u = x * (1.0 + sc_ref[0]) + sh_ref[0]
    ffn = routed + _swiglu(_bf16(u), ws13_ref[...], ws2_ref[...])
    res = ln_ref[2:3, :] * x + (1.0 + g_ref[0]) * ffn
    o_ref[...] = _layer_norm(res, ln_ref[0:1, :], ln_ref[1:2, :])


def _expert_layer(x, sh, sc, g, plan, slots, wts, w13, w2, ws13, ws2, ln, tm):
    bsz, s, d = x.shape
    per_b = s // tm
    n_tiles = bsz * per_b
    mod_spec = pl.BlockSpec((1, 1, d), lambda i, plan: (i // per_b, 0, 0))
    full2 = lambda i, plan: (0, 0)
    once = pl.Buffered(1)
    tile_smem = pl.BlockSpec((TOP_K * tm,), lambda i, plan: (i,), memory_space=pltpu.SMEM)
    slab_rows = (TOP_K * tm + EXPERT_ROWS) * SLAB_ROWS
    grid_spec = pltpu.PrefetchScalarGridSpec(
        num_scalar_prefetch=1,
        grid=(n_tiles,),
        in_specs=[
            tile_smem, tile_smem,
            pl.BlockSpec((tm, d), lambda i, plan: (i, 0), pipeline_mode=once),
            mod_spec, mod_spec, mod_spec,
            pl.BlockSpec(memory_space=pl.ANY),
            pl.BlockSpec(memory_space=pl.ANY),
            pl.BlockSpec(ws13.shape, full2, pipeline_mode=once),
            pl.BlockSpec(ws2.shape, full2, pipeline_mode=once),
            pl.BlockSpec(ln.shape, full2),
        ],
        out_specs=pl.BlockSpec((tm, d), lambda i, plan: (i, 0), pipeline_mode=once),
        scratch_shapes=[
            pltpu.VMEM((tm * SLAB_ROWS, LANES), jnp.float32),
            pltpu.VMEM((slab_rows, LANES), jnp.float32),
            pltpu.VMEM((W_SLOTS,) + w13.shape[1:], w13.dtype),
            pltpu.VMEM((W_SLOTS,) + w2.shape[1:], w2.dtype),
            pltpu.SemaphoreType.DMA((2, W_SLOTS)),
        ],
    )
    out = pl.pallas_call(
        _expert_kernel,
        grid_spec=grid_spec,
        out_shape=jax.ShapeDtypeStruct((bsz * s, d), jnp.float32),
        compiler_params=pltpu.CompilerParams(
            dimension_semantics=("arbitrary",),
            vmem_limit_bytes=VMEM_LIMIT_BYTES),
        name="moe_experts",
    )(plan.reshape(-1), slots.reshape(-1), wts.reshape(-1), x.reshape(bsz * s, d), sh, sc, g,
      w13, w2, ws13, ws2, ln)
    return out.reshape(bsz, s, d)


def _block_diag(w):
    h, n, _ = w.shape
    eye = jnp.eye(h, dtype=w.dtype)
    return (eye[:, None, :, None] * w[:, :, None, :]).reshape(h * n, h * n)


def _pad_rows(a, n_rows):
    return jnp.concatenate([a, jnp.zeros((n_rows - a.shape[0], a.shape[1]), a.dtype)], axis=0)


def _tile_rows(seq_len, want):
    t = min(want, seq_len)
    assert seq_len % t == 0 and t % RWKV_CHUNK == 0
    return t


def kernel(x, c, w_mod, b_mod, w_in, w_out, conv_a, conv_a_bias, ln_a_g, ln_a_b, conv_b, conv_b_bias, w_rg, b_rg, w_ig, b_ig, lru_lambda, mu_c, w0, w_w2, a0, w_a2, w_g2, k_k, k_a, r_k, gn_g, gn_b, conv_d, ln1_g, ln1_b, w_router, e_bias, w13, w2, ws13, ws2, ln2_g, ln2_b):
    n_layers = w_mod.shape[0]
    bsz, s, d = x.shape
    alpha_dn = (2.0 * n_layers) ** 0.25
    ts_mix = _tile_rows(s, 512)
    tm_moe = _tile_rows(s, 1024)
    bf = jnp.bfloat16

    mod = _modulation(c, w_mod, b_mod)
    bd = _block_diag(jnp.ones((N_HEADS, HEAD_DIM, HEAD_DIM), bf))
    alpha_row = jnp.full((1, d), alpha_dn, jnp.float32)
    tri_tok = jnp.triu(jnp.ones((tm_moe, tm_moe), bf), k=1)
    tri_exp = jnp.tril(jnp.ones((N_EXPERTS, N_EXPERTS), bf), k=-1)

    for l in range(n_layers):
        sh1, sc1, g1, sh2, sc2, g2 = [mod[l, :, i * d:(i + 1) * d].reshape(bsz, 1, d)
                                      for i in range(6)]
        cw = _pad_rows(jnp.concatenate([conv_a[l], conv_b[l], conv_d[l]], axis=0), N_CW_ROWS)
        vec = _pad_rows(jnp.stack([
            conv_a_bias[l], ln_a_g[l], ln_a_b[l], conv_b_bias[l], b_rg[l], b_ig[l], lru_lambda[l],
            w0[l], a0[l], k_k[l], k_a[l], r_k[l].reshape(D_GRP), gn_g[l], gn_b[l]], axis=0),
            N_VEC_ROWS)
        wgate = jnp.concatenate([_block_diag(w_rg[l]), _block_diag(w_ig[l])], axis=1).astype(bf)
        wlora = jnp.zeros((LORA_W + LORA_A + LORA_G, 3 * D_GRP), jnp.float32)
        wlora = wlora.at[0:LORA_W, 0:D_GRP].set(w_w2[l])
        wlora = wlora.at[LORA_W:LORA_W + LORA_A, D_GRP:2 * D_GRP].set(w_a2[l])
        wlora = wlora.at[LORA_W + LORA_A:, 2 * D_GRP:].set(w_g2[l]).astype(bf)
        ln1 = jnp.concatenate([ln1_g[l][None], ln1_b[l][None], alpha_row], axis=0)
        ln2 = jnp.concatenate([ln2_g[l][None], ln2_b[l][None], alpha_row], axis=0)

        x = _token_mixer_layer(x, sh1, sc1, g1, w_in[l].astype(bf), w_out[l].astype(bf), cw, vec,
                               mu_c[l][None], ln1, wgate, wlora, bd, ts_mix)
        slots, wts, plan = _router_layer(x, sh2, sc2, w_router[l].T, e_bias[l][:, None],
                                         tri_tok, tri_exp, tm_moe)
        x = _expert_layer(x, sh2, sc2, g2, plan, slots, wts, w13[l].astype(bf), w2[l].astype(bf),
                          ws13[l].astype(bf), ws2[l].astype(bf), ln2, tm_moe)
    return x
```

```python
import functools

import jax
import jax.numpy as jnp
from jax import lax
from jax.experimental import pallas as pl
from jax.experimental.pallas import tpu as pltpu

D_MODEL = 1024
HEAD_DIM = 64
D_GRP = 256
N_HEADS = D_GRP // HEAD_DIM
CONF_KERNEL = 31
LRU_CONV = 4
LRU_C = 8.0
SHORT_CONV = 3
LORA_W, LORA_A, LORA_G = 32, 32, 64
C_COLS = 3 * D_GRP + LORA_W + LORA_A + LORA_G
P_IN = 4 * D_GRP + C_COLS + 3 * D_GRP
RWKV_GN_EPS = 64e-5
N_EXPERTS = 64
TOP_K = 8
N_GROUPS = 8
TOPK_GROUPS = 4
E_PER_GROUP = N_EXPERTS // N_GROUPS
D_EXPERT = 256
ROUTED_SCALE = 2.5
LN_EPS = 1e-5

OFF_A = 0
OFF_B = 2 * D_GRP
OFF_C = 4 * D_GRP
OFF_D = OFF_C + C_COLS

SUBLANES = 8
LANES = 128
VMEM_LIMIT_BYTES = 58 * 1024 * 1024

SLAB_ROWS = D_MODEL // LANES
EXPERT_ROWS = 144
W_GROUP = 2
W_SLOTS = 2 * W_GROUP
RWKV_CHUNK = 64
CONF_HIST = 32
SMALL_HIST = 8

(V_CONV_A_BIAS, V_LN_A_G, V_LN_A_B, V_CONV_B_BIAS, V_B_RG, V_B_IG, V_LRU_LAMBDA, V_W0, V_A0,
 V_K_K, V_K_A, V_R_K, V_GN_G, V_GN_B) = range(14)
N_VEC_ROWS = 16
CW_A, CW_B, CW_D = 0, CONF_KERNEL, CONF_KERNEL + LRU_CONV
N_CW_ROWS = 40


def _bf16(x):
    return x.astype(jnp.bfloat16)


def _dot(a, b, dims=((1,), (0,))):
    return lax.dot_general(_bf16(a), _bf16(b), (dims, ((), ())),
                           preferred_element_type=jnp.float32)


def _split(a):
    hi = _bf16(a)
    lo = _bf16(a - hi.astype(jnp.float32))
    return hi, lo


def _dot_f32(a, b, dims=((1,), (0,))):
    a_hi, a_lo = _split(a)
    b_hi, b_lo = _split(b)
    dn = (dims, ((), ()))
    f = functools.partial(lax.dot_general, dimension_numbers=dn,
                          preferred_element_type=jnp.float32)
    return f(a_hi, b_hi) + (f(a_hi, b_lo) + f(a_lo, b_hi))


def _dot_lhs_f32(a, b_bf16):
    a_hi, a_lo = _split(a)
    f = functools.partial(jnp.dot, preferred_element_type=jnp.float32)
    return f(a_hi, b_bf16) + f(a_lo, b_bf16)


def _layer_norm(x, g, b):
    mu = jnp.mean(x, axis=-1, keepdims=True)
    xc = x - mu
    var = jnp.mean(xc * xc, axis=-1, keepdims=True)
    return xc * lax.rsqrt(var + LN_EPS) * g + b


def _softplus(x):
    return jnp.maximum(x, 0.0) + jnp.log1p(jnp.exp(-jnp.abs(x)))


def _row_iota(shape):
    return lax.broadcasted_iota(jnp.int32, shape, 0)


def _shift_rows(x, d, fill):
    rolled = pltpu.roll(x, d, axis=0)
    return jnp.where(_row_iota(x.shape) >= d, rolled, fill)


def _mod_kernel(c_ref, w_ref, b_ref, o_ref):
    c = c_ref[...]
    c_act = c * jax.nn.sigmoid(c)
    o_ref[0] = _dot_f32(c_act, w_ref[0]) + b_ref[0]


def _modulation(c, w_mod, b_mod):
    n_layers, d, d6 = w_mod.shape
    bsz = c.shape[0]
    tn = D_MODEL
    return pl.pallas_call(
        _mod_kernel,
        grid=(n_layers, d6 // tn),
        in_specs=[
            pl.BlockSpec((bsz, d), lambda l, j: (0, 0)),
            pl.BlockSpec((1, d, tn), lambda l, j: (l, 0, j)),
            pl.BlockSpec((1, 1, tn), lambda l, j: (l, 0, j)),
        ],
        out_specs=pl.BlockSpec((1, bsz, tn), lambda l, j: (l, 0, j)),
        out_shape=jax.ShapeDtypeStruct((n_layers, bsz, d6), jnp.float32),
        compiler_params=pltpu.CompilerParams(
            dimension_semantics=("arbitrary", "arbitrary"),
            vmem_limit_bytes=VMEM_LIMIT_BYTES),
        name="adaln_modulation",
    )(c, w_mod, b_mod.reshape(n_layers, 1, d6))


def _causal_conv(buf_ref, hist, ts, w_rows, n_taps):
    acc = None
    for k in range(n_taps):
        off = hist - (n_taps - 1) + k
        term = w_rows[k:k + 1, :] * buf_ref[pl.ds(off, ts), :]
        acc = term if acc is None else acc + term
    return acc


def _head_sum(x, bd_ref):
    return _dot_lhs_f32(x, bd_ref[...])


_NN = ((2,), (1,))
_NT = ((2,), (2,))
_TN = ((1,), (1,))


def _bdot(a, b, dims):
    return lax.dot_general(_bf16(a), _bf16(b), (dims, ((0,), (0,))),
                           preferred_element_type=jnp.float32)


def _to_problems(x):
    n_chunks = x.shape[0] // RWKV_CHUNK
    return jnp.stack([x[c * RWKV_CHUNK:(c + 1) * RWKV_CHUNK, h * HEAD_DIM:(h + 1) * HEAD_DIM]
                      for c in range(n_chunks) for h in range(N_HEADS)], axis=0)


def _rwkv_chunked(kk, r, alpha, kf, v, lw, state):
    ts = kk.shape[0]
    n_chunks = ts // RWKV_CHUNK
    cl = lw
    seg_row = _row_iota(cl.shape) % RWKV_CHUNK
    d = 1
    while d < RWKV_CHUNK:
        cl = cl + jnp.where(seg_row >= d, pltpu.roll(cl, d, axis=0), 0.0)
        d *= 2
    cl_end = cl.reshape(n_chunks, RWKV_CHUNK, D_GRP)[:, RWKV_CHUNK - 1:RWKV_CHUNK, :]
    cl_end_rows = jnp.broadcast_to(cl_end, (n_chunks, RWKV_CHUNK, D_GRP)).reshape(ts, D_GRP)
    g_inv = jnp.exp(-cl)
    tail = jnp.exp(cl_end_rows - cl)
    g_end = jnp.exp(cl_end)
    bt = _to_problems(kk * jnp.exp(cl - lw))
    rt = _to_problems(r * jnp.exp(cl))
    at = _to_problems(alpha * g_inv)
    kt = _to_problems(kf * g_inv)
    ab = _to_problems(alpha * tail)
    kb = _to_problems(kf * tail)
    vp = _to_problems(v)

    c_sz = RWKV_CHUNK
    tri_r = lax.broadcasted_iota(jnp.int32, (1, c_sz, c_sz), 1)
    tri_c = lax.broadcasted_iota(jnp.int32, (1, c_sz, c_sz), 2)
    strict = tri_r > tri_c
    incl = tri_r >= tri_c
    eye = (tri_r == tri_c).astype(jnp.float32)

    amat = _bdot(jnp.concatenate([bt, rt], axis=1), jnp.concatenate([at, kt], axis=1), _NT)
    a_ba = jnp.where(strict, amat[:, :c_sz, :c_sz], 0.0)
    a_bk = jnp.where(strict, amat[:, :c_sz, c_sz:], 0.0)
    a_ra = jnp.where(incl, amat[:, c_sz:, :c_sz], 0.0)
    a_rk = jnp.where(incl, amat[:, c_sz:, c_sz:], 0.0)
    tinv = eye + a_ba
    pw = a_ba
    step = 2
    while step < c_sz:
        pw = _bdot(pw, pw, _NN)
        tinv = tinv + _bdot(pw, tinv, _NN)
        step *= 2
    tx = _bdot(tinv, jnp.concatenate([bt, _bdot(a_bk, vp, _NN)], axis=2), _NN)
    w_t, u_t = tx[:, :, :HEAD_DIM], tx[:, :, HEAD_DIM:]
    y_k = _bdot(a_rk, vp, _NN)

    s_cur = state[...]
    y_rows = []
    for c in range(n_chunks):
        p = slice(c * N_HEADS, (c + 1) * N_HEADS)
        res = _bdot(jnp.concatenate([w_t[p], rt[p]], axis=1), s_cur, _NT)
        u_c = res[:, :c_sz, :] + u_t[p]
        y_c = res[:, c_sz:, :] + _bdot(a_ra[p], u_c, _NN) + y_k[p]
        g_c = jnp.stack([g_end[c, :, h * HEAD_DIM:(h + 1) * HEAD_DIM] for h in range(N_HEADS)],
                        axis=0)
        s_cur = s_cur * g_c + _bdot(jnp.concatenate([u_c, vp[p]], axis=1),
                                    jnp.concatenate([ab[p], kb[p]], axis=1), _TN)
        y_rows.append(jnp.concatenate([y_c[h] for h in range(N_HEADS)], axis=1))
    state[...] = s_cur
    return jnp.concatenate(y_rows, axis=0)


def _mixer_kernel(x_ref, sh_ref, sc_ref, g_ref, w_in_ref, w_out_ref, cw_ref, vec_ref, mu_ref,
                  ln_ref, wgate_ref, wlora_ref, bd_ref, o_ref,
                  hbuf, xbbuf, zbuf, pc_last, lru_h, state, ymix):
    ts = x_ref.shape[1]
    s_idx = pl.program_id(1)

    @pl.when(s_idx == 0)
    def _reset():
        hbuf[pl.ds(0, CONF_HIST), :] = jnp.zeros((CONF_HIST, D_GRP), jnp.float32)
        xbbuf[pl.ds(0, SMALL_HIST), :] = jnp.zeros((SMALL_HIST, D_GRP), jnp.float32)
        zbuf[pl.ds(0, SMALL_HIST), :] = jnp.zeros((SMALL_HIST, D_GRP), jnp.float32)
        pc_last[...] = jnp.zeros(pc_last.shape, jnp.float32)
        lru_h[...] = jnp.zeros(lru_h.shape, jnp.float32)
        state[...] = jnp.zeros(state.shape, jnp.float32)

    x = x_ref[0]
    u = x * (1.0 + sc_ref[0]) + sh_ref[0]
    proj = jnp.dot(_bf16(u), w_in_ref[...], preferred_element_type=jnp.float32)

    vec = vec_ref[...]

    def vrow(i):
        return vec[i:i + 1, :]

    cw = cw_ref[...]

    val = proj[:, OFF_A:OFF_A + D_GRP]
    gate = proj[:, OFF_A + D_GRP:OFF_A + 2 * D_GRP]
    hbuf[pl.ds(CONF_HIST, ts), :] = val * jax.nn.sigmoid(gate)
    conv = _causal_conv(hbuf, CONF_HIST, ts, cw[CW_A:CW_A + CONF_KERNEL], CONF_KERNEL)
    conv = conv + vrow(V_CONV_A_BIAS)
    hbuf[pl.ds(0, CONF_HIST), :] = hbuf[pl.ds(ts, CONF_HIST), :]
    ln_a = _layer_norm(conv, vrow(V_LN_A_G), vrow(V_LN_A_B))
    ymix[:, 0:D_GRP] = ln_a * jax.nn.sigmoid(ln_a)

    xbbuf[pl.ds(SMALL_HIST, ts), :] = proj[:, OFF_B:OFF_B + D_GRP]
    gb = proj[:, OFF_B + D_GRP:OFF_B + 2 * D_GRP]
    ub = _causal_conv(xbbuf, SMALL_HIST, ts, cw[CW_B:CW_B + LRU_CONV], LRU_CONV)
    ub = ub + vrow(V_CONV_B_BIAS)
    xbbuf[pl.ds(0, SMALL_HIST), :] = xbbuf[pl.ds(ts, SMALL_HIST), :]
    gates = jnp.dot(_bf16(ub), wgate_ref[...], preferred_element_type=jnp.float32)
    r_gate = jax.nn.sigmoid(gates[:, :D_GRP] + vrow(V_B_RG))
    i_gate = jax.nn.sigmoid(gates[:, D_GRP:] + vrow(V_B_IG))
    log_a = (-LRU_C) * r_gate * _softplus(-vrow(V_LRU_LAMBDA))
    a_sc = jnp.exp(log_a)
    b_sc = jnp.sqrt(-jnp.tanh(log_a) * (a_sc * a_sc + 1.0)) * (i_gate * ub)
    d = 1
    while d < ts:
        a_sh = _shift_rows(a_sc, d, 1.0)
        b_sh = _shift_rows(b_sc, d, 0.0)
        b_sc = a_sc * b_sh + b_sc
        a_sc = a_sc * a_sh
        d *= 2
    h = b_sc + a_sc * lru_h[0:1, :]
    lru_h[0:1, :] = h[ts - 1:ts, :]
    ymix[:, D_GRP:2 * D_GRP] = h * jax.nn.gelu(gb, approximate=True)

    gbd = proj[:, OFF_D:OFF_D + D_GRP]
    zbuf[pl.ds(SMALL_HIST, ts), :] = (proj[:, OFF_D + D_GRP:OFF_D + 2 * D_GRP]
                                      * proj[:, OFF_D + 2 * D_GRP:OFF_D + 3 * D_GRP])
    convd = _causal_conv(zbuf, SMALL_HIST, ts, cw[CW_D:CW_D + SHORT_CONV], SHORT_CONV)
    zbuf[pl.ds(0, SMALL_HIST), :] = zbuf[pl.ds(ts, SMALL_HIST), :]
    ymix[:, 3 * D_GRP:4 * D_GRP] = gbd * convd

    pc = proj[:, OFF_C:OFF_C + C_COLS]
    prev = jnp.where(_row_iota(pc.shape) == 0, pc_last[0:1, :], pltpu.roll(pc, 1, axis=0))
    pc_last[0:1, :] = pc[ts - 1:ts, :]
    xs = pc + (prev - pc) * mu_ref[...]
    r = xs[:, 0:D_GRP]
    k = xs[:, D_GRP:2 * D_GRP]
    v = xs[:, 2 * D_GRP:3 * D_GRP]
    z = xs[:, 3 * D_GRP:C_COLS]
    lane = lax.broadcasted_iota(jnp.int32, z.shape, 1)
    z_act = jnp.where(lane < LORA_W, jnp.tanh(z),
                      jnp.where(lane < LORA_W + LORA_A, z, jax.nn.sigmoid(z)))
    lora = jnp.dot(_bf16(z_act), wlora_ref[...], preferred_element_type=jnp.float32)
    wlog = -_softplus(-(vrow(V_W0) + lora[:, 0:D_GRP])) - 0.5
    lw = -jnp.exp(wlog)
    a_icl = jax.nn.sigmoid(vrow(V_A0) + lora[:, D_GRP:2 * D_GRP])
    g_out = lora[:, 2 * D_GRP:3 * D_GRP]

    kk = k * vrow(V_K_K)
    kk = kk * lax.rsqrt(_head_sum(kk * kk, bd_ref) + 1e-12)
    kf = k * (1.0 + (a_icl - 1.0) * vrow(V_K_A))

    y = _rwkv_chunked(kk, r, -(kk * a_icl), kf, v, lw, state)
    inv_n = 1.0 / HEAD_DIM
    mu_y = _head_sum(y, bd_ref) * inv_n
    yc = y - mu_y
    var_y = _head_sum(yc * yc, bd_ref) * inv_n
    y = yc * lax.rsqrt(var_y + RWKV_GN_EPS) * vrow(V_GN_G) + vrow(V_GN_B)
    bonus = _head_sum(r * kf * vrow(V_R_K), bd_ref) * v
    ymix[:, 2 * D_GRP:3 * D_GRP] = (y + bonus) * g_out

    mix = jnp.dot(_bf16(ymix[...]), w_out_ref[...], preferred_element_type=jnp.float32)
    alpha_dn = ln_ref[2:3, :]
    res = alpha_dn * x + (1.0 + g_ref[0]) * mix
    o_ref[0] = _layer_norm(res, ln_ref[0:1, :], ln_ref[1:2, :])


def _token_mixer_layer(x, sh, sc, g, w_in, w_out, cw, vec, mu, ln, wgate, wlora, bd, ts):
    bsz, s, d = x.shape
    grid = (bsz, s // ts)
    full2 = lambda b, i: (0, 0)
    mod_spec = pl.BlockSpec((1, 1, d), lambda b, i: (b, 0, 0))
    f32 = jnp.float32

    def tile(n_cols):
        return pltpu.VMEM((ts, n_cols), f32)

    return pl.pallas_call(
        _mixer_kernel,
        grid=grid,
        in_specs=[
            pl.BlockSpec((1, ts, d), lambda b, i: (b, i, 0)),
            mod_spec, mod_spec, mod_spec,
            pl.BlockSpec(w_in.shape, full2),
            pl.BlockSpec(w_out.shape, full2),
            pl.BlockSpec(cw.shape, full2),
            pl.BlockSpec(vec.shape, full2),
            pl.BlockSpec(mu.shape, full2),
            pl.BlockSpec(ln.shape, full2),
            pl.BlockSpec(wgate.shape, full2),
            pl.BlockSpec(wlora.shape, full2),
            pl.BlockSpec(bd.shape, full2),
        ],
        out_specs=pl.BlockSpec((1, ts, d), lambda b, i: (b, i, 0)),
        out_shape=jax.ShapeDtypeStruct((bsz, s, d), f32),
        scratch_shapes=[
            pltpu.VMEM((CONF_HIST + ts, D_GRP), f32),
            pltpu.VMEM((SMALL_HIST + ts, D_GRP), f32),
            pltpu.VMEM((SMALL_HIST + ts, D_GRP), f32),
            pltpu.VMEM((SUBLANES, C_COLS), f32),
            pltpu.VMEM((SUBLANES, D_GRP), f32),
            pltpu.VMEM((N_HEADS, HEAD_DIM, HEAD_DIM), f32),
            tile(D_MODEL),
        ],
        compiler_params=pltpu.CompilerParams(
            dimension_semantics=("arbitrary", "arbitrary"),
            vmem_limit_bytes=VMEM_LIMIT_BYTES),
        name="token_mixers",
    )(x, sh, sc, g, w_in, w_out, cw, vec, mu, ln, wgate, wlora, bd)


def _first_index(mask, idx, sentinel):
    return jnp.min(jnp.where(mask, idx, sentinel), axis=0, keepdims=True)


def _router_kernel(x_ref, sh_ref, sc_ref, wr_ref, eb_ref, su_ref, sl_ref,
                   slot_ref, wt_ref, plan_ref):
    x = x_ref[0]
    u = x * (1.0 + sc_ref[0]) + sh_ref[0]
    tm = u.shape[0]
    logits = _dot_f32(wr_ref[...], u, ((1,), (1,)))
    scores = jax.nn.sigmoid(logits)
    biased = scores + eb_ref[...]
    neg_inf = jnp.float32(-jnp.inf)
    sub = lax.broadcasted_iota(jnp.int32, (E_PER_GROUP, tm), 0).astype(jnp.float32)

    groups = [biased[g * E_PER_GROUP:(g + 1) * E_PER_GROUP, :] for g in range(N_GROUPS)]
    gscore = []
    for blk in groups:
        m1 = jnp.max(blk, axis=0, keepdims=True)
        first = _first_index(blk == m1, sub, float(E_PER_GROUP))
        m2 = jnp.max(jnp.where(sub == first, neg_inf, blk), axis=0, keepdims=True)
        gscore.append(m1 + m2)
    gs = jnp.concatenate(gscore, axis=0)
    gidx = lax.broadcasted_iota(jnp.int32, (N_GROUPS, tm), 0).astype(jnp.float32)
    keep = jnp.zeros((N_GROUPS, tm), jnp.float32)
    for _ in range(TOPK_GROUPS):
        m = jnp.max(gs, axis=0, keepdims=True)
        first = _first_index(gs == m, gidx, float(N_GROUPS))
        sel = gidx == first
        keep = jnp.where(sel, 1.0, keep)
        gs = jnp.where(sel, neg_inf, gs)

    masked = [jnp.where(keep[g:g + 1, :] > 0.5, groups[g], neg_inf) for g in range(N_GROUPS)]
    eidx = [sub + float(g * E_PER_GROUP) for g in range(N_GROUPS)]
    chosen = [jnp.zeros((E_PER_GROUP, tm), jnp.float32) for _ in range(N_GROUPS)]
    for _ in range(TOP_K):
        m = functools.reduce(jnp.maximum, [jnp.max(b, axis=0, keepdims=True) for b in masked])
        first = functools.reduce(
            jnp.minimum,
            [_first_index(masked[g] == m, eidx[g], float(N_EXPERTS)) for g in range(N_GROUPS)])
        for g in range(N_GROUPS):
            sel = eidx[g] == first
            chosen[g] = jnp.where(sel, 1.0, chosen[g])
            masked[g] = jnp.where(sel, neg_inf, masked[g])

    picked = [jnp.where(chosen[g] > 0.5, scores[g * E_PER_GROUP:(g + 1) * E_PER_GROUP, :], 0.0)
              for g in range(N_GROUPS)]
    total = functools.reduce(jnp.add, [jnp.sum(p, axis=0, keepdims=True) for p in picked])
    wts = jnp.concatenate([p / total * ROUTED_SCALE for p in picked], axis=0)

    sel = jnp.concatenate(chosen, axis=0)
    sel_bf = _bf16(sel)
    rank = jnp.dot(sel_bf, su_ref[...], preferred_element_type=jnp.float32)
    cnt = jnp.sum(sel, axis=1, keepdims=True)
    off = _dot_f32(sl_ref[...].astype(jnp.float32), jnp.broadcast_to(cnt, (N_EXPERTS, LANES)))
    slot = off[:, 0:1] + rank
    kidx = jnp.dot(sl_ref[...], sel_bf, preferred_element_type=jnp.float32)
    slot_rows, wt_rows = [], []
    for kk in range(TOP_K):
        pick = (sel > 0.5) & (kidx == float(kk))
        slot_rows.append(jnp.sum(jnp.where(pick, slot, 0.0), axis=0, keepdims=True))
        wt_rows.append(jnp.sum(jnp.where(pick, wts, 0.0), axis=0, keepdims=True))
    lane_pad = [jnp.zeros((LANES - TOP_K, tm), jnp.float32)]
    slot_ref[0] = (jnp.concatenate(slot_rows + lane_pad, axis=0).T
                   * float(SLAB_ROWS)).astype(jnp.int32)
    wt_ref[0] = jnp.concatenate(wt_rows + lane_pad, axis=0).T
    e_sub = lax.broadcasted_iota(jnp.int32, (N_EXPERTS, LANES), 0)
    e_lane = lax.broadcasted_iota(jnp.int32, (N_EXPERTS, LANES), 1)
    diag = e_sub == e_lane
    off_row = jnp.sum(jnp.where(diag, off, 0.0), axis=0, keepdims=True)
    cnt_row = jnp.sum(jnp.where(diag, jnp.broadcast_to(cnt, (N_EXPERTS, LANES)), 0.0),
                      axis=0, keepdims=True)
    plan_ref[0] = jnp.concatenate([off_row, cnt_row], axis=0).astype(jnp.int32)


def _router_layer(x, sh, sc, w_router_t, e_bias_col, su, sl, tm):
    bsz, s, d = x.shape
    per_b = s // tm
    n_tiles = bsz * per_b
    mod_spec = pl.BlockSpec((1, 1, d), lambda b, i: (b, 0, 0))
    const2 = lambda b, i: (0, 0)
    tile3 = lambda b, i: (b * per_b + i, 0, 0)
    return pl.pallas_call(
        _router_kernel,
        grid=(bsz, per_b),
        in_specs=[
            pl.BlockSpec((1, tm, d), lambda b, i: (b, i, 0)),
            mod_spec, mod_spec,
            pl.BlockSpec(w_router_t.shape, const2),
            pl.BlockSpec(e_bias_col.shape, const2),
            pl.BlockSpec(su.shape, const2),
            pl.BlockSpec(sl.shape, const2),
        ],
        out_specs=[
            pl.BlockSpec((1, tm, LANES), tile3),
            pl.BlockSpec((1, tm, LANES), tile3),
            pl.BlockSpec((1, 2, LANES), tile3),
        ],
        out_shape=[
            jax.ShapeDtypeStruct((n_tiles, tm, LANES), jnp.int32),
            jax.ShapeDtypeStruct((n_tiles, tm, LANES), jnp.float32),
            jax.ShapeDtypeStruct((n_tiles, 2, LANES), jnp.int32),
        ],
        compiler_params=pltpu.CompilerParams(
            dimension_semantics=("arbitrary", "arbitrary"),
            vmem_limit_bytes=VMEM_LIMIT_BYTES),
        name="moe_router",
    )(x, sh, sc, w_router_t, e_bias_col, su, sl)


def _swiglu(u_bf16, w13, w2):
    hcat = jnp.dot(u_bf16, w13, preferred_element_type=jnp.float32)
    half = hcat.shape[1] // 2
    gate, up = hcat[:, :half], hcat[:, half:]
    act = gate * jax.nn.sigmoid(gate) * up
    return jnp.dot(_bf16(act), w2, preferred_element_type=jnp.float32)


def _weight_copies(w13_hbm, w2_hbm, w13_buf, w2_buf, sems, e, slot):
    return (pltpu.make_async_copy(w13_hbm.at[e], w13_buf.at[slot], sems.at[0, slot]),
            pltpu.make_async_copy(w2_hbm.at[e], w2_buf.at[slot], sems.at[1, slot]))


def _expert_kernel(plan_ref, slot_ref, wt_ref, x_ref, sh_ref, sc_ref, g_ref, w13_hbm, w2_hbm,
                   ws13_ref, ws2_ref, ln_ref, o_ref, rows_f, xf, w13_buf, w2_buf, sems):
    i = pl.program_id(0)
    tm = x_ref.shape[0]
    n_exp = w13_hbm.shape[0]
    n_col = x_ref.shape[1] // LANES

    @pl.when(i == 0)
    def _clear():
        xf[...] = jnp.zeros(xf.shape, xf.dtype)

    for e0 in range(W_GROUP):
        for cp in _weight_copies(w13_hbm, w2_hbm, w13_buf, w2_buf, sems, e0, e0):
            cp.start()

    u = x_ref[...] * (1.0 + sc_ref[0]) + sh_ref[0]
    for j in range(n_col):
        rows_f[pl.ds(j, tm, stride=SLAB_ROWS), :] = u[:, j * LANES:(j + 1) * LANES]

    def scatter(t, carry):
        row = rows_f[pl.ds(pl.multiple_of(t * SLAB_ROWS, SLAB_ROWS), SLAB_ROWS), :]
        for k in range(TOP_K):
            dst = pl.multiple_of(slot_ref[t * TOP_K + k], SLAB_ROWS)
            xf[pl.ds(dst, SLAB_ROWS), :] = row
        return carry

    lax.fori_loop(0, tm, scatter, 0)

    def load_block(first, j):
        base = (first + j * EXPERT_ROWS) * SLAB_ROWS
        return [xf[pl.ds(base + q, EXPERT_ROWS, stride=SLAB_ROWS), :] for q in range(n_col)]

    def store_block(first, count, j, old, y):
        base = (first + j * EXPERT_ROWS) * SLAB_ROWS
        live = lax.broadcasted_iota(jnp.int32, (EXPERT_ROWS, LANES), 0) < count - j * EXPERT_ROWS
        for q in range(n_col):
            xf[pl.ds(base + q, EXPERT_ROWS, stride=SLAB_ROWS), :] = jnp.where(
                live, y[:, q * LANES:(q + 1) * LANES], old[q])

    def transform(slot, old):
        xb = jnp.concatenate([_bf16(o) for o in old], axis=1)
        return _swiglu(xb, w13_buf[slot], w2_buf[slot])

    def run_block(slot, first, count, j):
        old = load_block(first, j)
        store_block(first, count, j, old, transform(slot, old))

    def expert_group(p, carry):
        e_lo = p * W_GROUP
        members = []
        for h in range(W_GROUP):
            e = e_lo + h
            slot = lax.rem(e, W_SLOTS)
            for cp in _weight_copies(w13_hbm, w2_hbm, w13_buf, w2_buf, sems, e, slot):
                cp.wait()
            members.append((slot, plan_ref[i * 2 * LANES + e], plan_ref[i * 2 * LANES + LANES + e]))

        @pl.when(e_lo + W_GROUP < n_exp)
        def _prefetch():
            for h in range(W_GROUP):
                nxt = e_lo + W_GROUP + h
                for cp in _weight_copies(w13_hbm, w2_hbm, w13_buf, w2_buf, sems, nxt,
                                         lax.rem(nxt, W_SLOTS)):
                    cp.start()

        olds = [load_block(first, 0) for _, first, _ in members]
        ys = [transform(slot, old) for (slot, _, _), old in zip(members, olds)]
        for (_, first, count), old, y in zip(members, olds, ys):
            store_block(first, count, 0, old, y)
        for slot, first, count in members:
            n_blocks = lax.div(count + (EXPERT_ROWS - 1), EXPERT_ROWS)
            lax.fori_loop(1, n_blocks,
                          lambda j, c, s=slot, f=first, n=count: (run_block(s, f, n, j), c)[1], 0)
        return carry

    lax.fori_loop(0, n_exp // W_GROUP, expert_group, 0)

    def gather(t, carry):
        acc = jnp.zeros((SLAB_ROWS, LANES), jnp.float32)
        for k in range(TOP_K):
            src = pl.multiple_of(slot_ref[t * TOP_K + k], SLAB_ROWS)
            acc = acc + wt_ref[t * TOP_K + k] * xf[pl.ds(src, SLAB_ROWS), :]
        rows_f[pl.ds(pl.multiple_of(t * SLAB_ROWS, SLAB_ROWS), SLAB_ROWS), :] = acc
        return carry

    lax.fori_loop(0, tm, gather, 0)
    routed = jnp.concatenate([rows_f[pl.ds(j, tm, stride=SLAB_ROWS), :] for j in range(n_col)],
                             axis=1)
    x = x_ref[...]
    u = x * (1.0 + sc_ref[0]) + sh_ref[0]
    ffn = routed + _swiglu(_bf16(u), ws13_ref[...], ws2_ref[...])
    res = ln_ref[2:3, :] * x + (1.0 + g_ref[0]) * ffn
    o_ref[...] = _layer_norm(res, ln_ref[0:1, :], ln_ref[1:2, :])


def _expert_layer(x, sh, sc, g, plan, slots, wts, w13, w2, ws13, ws2, ln, tm):
    bsz, s, d = x.shape
    per_b = s // tm
    n_tiles = bsz * per_b
    mod_spec = pl.BlockSpec((1, 1, d), lambda i, plan: (i // per_b, 0, 0))
    full2 = lambda i, plan: (0, 0)
    once = pl.Buffered(1)
    tile_smem = pl.BlockSpec((TOP_K * tm,), lambda i, plan: (i,), memory_space=pltpu.SMEM)
    slab_rows = (TOP_K * tm + EXPERT_ROWS) * SLAB_ROWS
    grid_spec = pltpu.PrefetchScalarGridSpec(
        num_scalar_prefetch=1,
        grid=(n_tiles,),
        in_specs=[
            tile_smem, tile_smem,
            pl.BlockSpec((tm, d), lambda i, plan: (i, 0), pipeline_mode=once),
            mod_spec, mod_spec, mod_spec,
            pl.BlockSpec(memory_space=pl.ANY),
            pl.BlockSpec(memory_space=pl.ANY),
            pl.BlockSpec(ws13.shape, full2, pipeline_mode=once),
            pl.BlockSpec(ws2.shape, full2, pipeline_mode=once),
            pl.BlockSpec(ln.shape, full2),
        ],
        out_specs=pl.BlockSpec((tm, d), lambda i, plan: (i, 0), pipeline_mode=once),
        scratch_shapes=[
            pltpu.VMEM((tm * SLAB_ROWS, LANES), jnp.float32),
            pltpu.VMEM((slab_rows, LANES), jnp.float32),
            pltpu.VMEM((W_SLOTS,) + w13.shape[1:], w13.dtype),
            pltpu.VMEM((W_SLOTS,) + w2.shape[1:], w2.dtype),
            pltpu.SemaphoreType.DMA((2, W_SLOTS)),
        ],
    )
    out = pl.pallas_call(
        _expert_kernel,
        grid_spec=grid_spec,
        out_shape=jax.ShapeDtypeStruct((bsz * s, d), jnp.float32),
        compiler_params=pltpu.CompilerParams(
            dimension_semantics=("arbitrary",),
            vmem_limit_bytes=VMEM_LIMIT_BYTES),
        name="moe_experts",
    )(plan.reshape(-1), slots[:, :, :TOP_K].reshape(-1), wts[:, :, :TOP_K].reshape(-1),
      x.reshape(bsz * s, d), sh, sc, g,
      w13, w2, ws13, ws2, ln)
    return out.reshape(bsz, s, d)


def _block_diag(w):
    h, n, _ = w.shape
    eye = jnp.eye(h, dtype=w.dtype)
    return (eye[:, None, :, None] * w[:, :, None, :]).reshape(h * n, h * n)


def _pad_rows(a, n_rows):
    return jnp.concatenate([a, jnp.zeros((n_rows - a.shape[0], a.shape[1]), a.dtype)], axis=0)


def _tile_rows(seq_len, want):
    t = min(want, seq_len)
    assert seq_len % t == 0 and t % RWKV_CHUNK == 0
    return t


def kernel(x, c, w_mod, b_mod, w_in, w_out, conv_a, conv_a_bias, ln_a_g, ln_a_b, conv_b, conv_b_bias, w_rg, b_rg, w_ig, b_ig, lru_lambda, mu_c, w0, w_w2, a0, w_a2, w_g2, k_k, k_a, r_k, gn_g, gn_b, conv_d, ln1_g, ln1_b, w_router, e_bias, w13, w2, ws13, ws2, ln2_g, ln2_b):
    n_layers = w_mod.shape[0]
    bsz, s, d = x.shape
    alpha_dn = (2.0 * n_layers) ** 0.25
    ts_mix = _tile_rows(s, 512)
    tm_moe = _tile_rows(s, 1024)
    bf = jnp.bfloat16

    mod = _modulation(c, w_mod, b_mod)
    bd = _block_diag(jnp.ones((N_HEADS, HEAD_DIM, HEAD_DIM), bf))
    alpha_row = jnp.full((1, d), alpha_dn, jnp.float32)
    tri_tok = jnp.triu(jnp.ones((tm_moe, tm_moe), bf), k=1)
    tri_exp = jnp.tril(jnp.ones((N_EXPERTS, N_EXPERTS), bf), k=-1)

    for l in range(n_layers):
        sh1, sc1, g1, sh2, sc2, g2 = [mod[l, :, i * d:(i + 1) * d].reshape(bsz, 1, d)
                                      for i in range(6)]
        cw = _pad_rows(jnp.concatenate([conv_a[l], conv_b[l], conv_d[l]], axis=0), N_CW_ROWS)
        vec = _pad_rows(jnp.stack([
            conv_a_bias[l], ln_a_g[l], ln_a_b[l], conv_b_bias[l], b_rg[l], b_ig[l], lru_lambda[l],
            w0[l], a0[l], k_k[l], k_a[l], r_k[l].reshape(D_GRP), gn_g[l], gn_b[l]], axis=0),
            N_VEC_ROWS)
        wgate = jnp.concatenate([_block_diag(w_rg[l]), _block_diag(w_ig[l])], axis=1).astype(bf)
        wlora = jnp.zeros((LORA_W + LORA_A + LORA_G, 3 * D_GRP), jnp.float32)
        wlora = wlora.at[0:LORA_W, 0:D_GRP].set(w_w2[l])
        wlora = wlora.at[LORA_W:LORA_W + LORA_A, D_GRP:2 * D_GRP].set(w_a2[l])
        wlora = wlora.at[LORA_W + LORA_A:, 2 * D_GRP:].set(w_g2[l]).astype(bf)
        ln1 = jnp.concatenate([ln1_g[l][None], ln1_b[l][None], alpha_row], axis=0)
        ln2 = jnp.concatenate([ln2_g[l][None], ln2_b[l][None], alpha_row], axis=0)

        x = _token_mixer_layer(x, sh1, sc1, g1, w_in[l].astype(bf), w_out[l].astype(bf), cw, vec,
                               mu_c[l][None], ln1, wgate, wlora, bd, ts_mix)
        slots, wts, plan = _router_layer(x, sh2, sc2, w_router[l].T, e_bias[l][:, None],
                                         tri_tok, tri_exp, tm_moe)
        x = _expert_layer(x, sh2, sc2, g2, plan, slots, wts, w13[l].astype(bf), w2[l].astype(bf),
                          ws13[l].astype(bf), ws2[l].astype(bf), ln2, tm_moe)
    return x
```

```python
import functools

import jax
import jax.numpy as jnp
from jax import lax
from jax.experimental import pallas as pl
from jax.experimental.pallas import tpu as pltpu

D_MODEL = 1024
HEAD_DIM = 64
D_GRP = 256
N_HEADS = D_GRP // HEAD_DIM
CONF_KERNEL = 31
LRU_CONV = 4
LRU_C = 8.0
SHORT_CONV = 3
LORA_W, LORA_A, LORA_G = 32, 32, 64
C_COLS = 3 * D_GRP + LORA_W + LORA_A + LORA_G
P_IN = 4 * D_GRP + C_COLS + 3 * D_GRP
RWKV_GN_EPS = 64e-5
N_EXPERTS = 64
TOP_K = 8
N_GROUPS = 8
TOPK_GROUPS = 4
E_PER_GROUP = N_EXPERTS // N_GROUPS
D_EXPERT = 256
ROUTED_SCALE = 2.5
LN_EPS = 1e-5

OFF_A = 0
OFF_B = 2 * D_GRP
OFF_C = 4 * D_GRP
OFF_D = OFF_C + C_COLS

SUBLANES = 8
LANES = 128
VMEM_LIMIT_BYTES = 58 * 1024 * 1024

SLAB_ROWS = D_MODEL // LANES
EXPERT_ROWS = 144
W_SLOTS = 4
W_AHEAD = W_SLOTS - 1
RWKV_CHUNK = 64
CONF_HIST = 32
SMALL_HIST = 8

(V_CONV_A_BIAS, V_LN_A_G, V_LN_A_B, V_CONV_B_BIAS, V_B_RG, V_B_IG, V_LRU_LAMBDA, V_W0, V_A0,
 V_K_K, V_K_A, V_R_K, V_GN_G, V_GN_B) = range(14)
N_VEC_ROWS = 16
CW_A, CW_B, CW_D = 0, CONF_KERNEL, CONF_KERNEL + LRU_CONV
N_CW_ROWS = 40


def _bf16(x):
    return x.astype(jnp.bfloat16)


def _dot(a, b, dims=((1,), (0,))):
    return lax.dot_general(_bf16(a), _bf16(b), (dims, ((), ())),
                           preferred_element_type=jnp.float32)


def _split(a):
    hi = _bf16(a)
    lo = _bf16(a - hi.astype(jnp.float32))
    return hi, lo


def _dot_f32(a, b, dims=((1,), (0,))):
    a_hi, a_lo = _split(a)
    b_hi, b_lo = _split(b)
    dn = (dims, ((), ()))
    f = functools.partial(lax.dot_general, dimension_numbers=dn,
                          preferred_element_type=jnp.float32)
    return f(a_hi, b_hi) + (f(a_hi, b_lo) + f(a_lo, b_hi))


def _dot_lhs_f32(a, b_bf16):
    a_hi, a_lo = _split(a)
    f = functools.partial(jnp.dot, preferred_element_type=jnp.float32)
    return f(a_hi, b_bf16) + f(a_lo, b_bf16)


def _layer_norm(x, g, b):
    mu = jnp.mean(x, axis=-1, keepdims=True)
    xc = x - mu
    var = jnp.mean(xc * xc, axis=-1, keepdims=True)
    return xc * lax.rsqrt(var + LN_EPS) * g + b


def _softplus(x):
    return jnp.maximum(x, 0.0) + jnp.log1p(jnp.exp(-jnp.abs(x)))


def _row_iota(shape):
    return lax.broadcasted_iota(jnp.int32, shape, 0)


def _shift_rows(x, d, fill):
    rolled = pltpu.roll(x, d, axis=0)
    return jnp.where(_row_iota(x.shape) >= d, rolled, fill)


def _mod_kernel(c_ref, w_ref, b_ref, o_ref):
    c = c_ref[...]
    c_act = c * jax.nn.sigmoid(c)
    o_ref[0] = _dot_f32(c_act, w_ref[0]) + b_ref[0]


def _modulation(c, w_mod, b_mod):
    n_layers, d, d6 = w_mod.shape
    bsz = c.shape[0]
    tn = D_MODEL
    return pl.pallas_call(
        _mod_kernel,
        grid=(n_layers, d6 // tn),
        in_specs=[
            pl.BlockSpec((bsz, d), lambda l, j: (0, 0)),
            pl.BlockSpec((1, d, tn), lambda l, j: (l, 0, j)),
            pl.BlockSpec((1, 1, tn), lambda l, j: (l, 0, j)),
        ],
        out_specs=pl.BlockSpec((1, bsz, tn), lambda l, j: (l, 0, j)),
        out_shape=jax.ShapeDtypeStruct((n_layers, bsz, d6), jnp.float32),
        compiler_params=pltpu.CompilerParams(
            dimension_semantics=("arbitrary", "arbitrary"),
            vmem_limit_bytes=VMEM_LIMIT_BYTES),
        name="adaln_modulation",
    )(c, w_mod, b_mod.reshape(n_layers, 1, d6))


def _causal_conv(buf_ref, hist, ts, w_rows, n_taps):
    acc = None
    for k in range(n_taps):
        off = hist - (n_taps - 1) + k
        term = w_rows[k:k + 1, :] * buf_ref[pl.ds(off, ts), :]
        acc = term if acc is None else acc + term
    return acc


def _head_sum(x, bd_ref):
    return _dot_lhs_f32(x, bd_ref[...])


_NN = ((2,), (1,))
_NT = ((2,), (2,))
_TN = ((1,), (1,))


def _bdot(a, b, dims):
    return lax.dot_general(_bf16(a), _bf16(b), (dims, ((0,), (0,))),
                           preferred_element_type=jnp.float32)


def _to_problems(x):
    n_chunks = x.shape[0] // RWKV_CHUNK
    return jnp.stack([x[c * RWKV_CHUNK:(c + 1) * RWKV_CHUNK, h * HEAD_DIM:(h + 1) * HEAD_DIM]
                      for c in range(n_chunks) for h in range(N_HEADS)], axis=0)


def _rwkv_chunked(kk, r, alpha, kf, v, lw, state):
    ts = kk.shape[0]
    n_chunks = ts // RWKV_CHUNK
    cl = lw
    seg_row = _row_iota(cl.shape) % RWKV_CHUNK
    d = 1
    while d < RWKV_CHUNK:
        cl = cl + jnp.where(seg_row >= d, pltpu.roll(cl, d, axis=0), 0.0)
        d *= 2
    cl_end = cl.reshape(n_chunks, RWKV_CHUNK, D_GRP)[:, RWKV_CHUNK - 1:RWKV_CHUNK, :]
    cl_end_rows = jnp.broadcast_to(cl_end, (n_chunks, RWKV_CHUNK, D_GRP)).reshape(ts, D_GRP)
    g_inv = jnp.exp(-cl)
    tail = jnp.exp(cl_end_rows - cl)
    g_end = jnp.exp(cl_end)
    bt = _to_problems(kk * jnp.exp(cl - lw))
    rt = _to_problems(r * jnp.exp(cl))
    at = _to_problems(alpha * g_inv)
    kt = _to_problems(kf * g_inv)
    ab = _to_problems(alpha * tail)
    kb = _to_problems(kf * tail)
    vp = _to_problems(v)

    c_sz = RWKV_CHUNK
    tri_r = lax.broadcasted_iota(jnp.int32, (1, c_sz, c_sz), 1)
    tri_c = lax.broadcasted_iota(jnp.int32, (1, c_sz, c_sz), 2)
    strict = tri_r > tri_c
    incl = tri_r >= tri_c
    eye = (tri_r == tri_c).astype(jnp.float32)

    amat = _bdot(jnp.concatenate([bt, rt], axis=1), jnp.concatenate([at, kt], axis=1), _NT)
    a_ba = jnp.where(strict, amat[:, :c_sz, :c_sz], 0.0)
    a_bk = jnp.where(strict, amat[:, :c_sz, c_sz:], 0.0)
    a_ra = jnp.where(incl, amat[:, c_sz:, :c_sz], 0.0)
    a_rk = jnp.where(incl, amat[:, c_sz:, c_sz:], 0.0)
    tinv = eye + a_ba
    pw = a_ba
    step = 2
    while step < c_sz:
        pw = _bdot(pw, pw, _NN)
        tinv = tinv + _bdot(pw, tinv, _NN)
        step *= 2
    tx = _bdot(tinv, jnp.concatenate([bt, _bdot(a_bk, vp, _NN)], axis=2), _NN)
    w_t, u_t = tx[:, :, :HEAD_DIM], tx[:, :, HEAD_DIM:]
    y_k = _bdot(a_rk, vp, _NN)

    s_cur = state[...]
    y_rows = []
    for c in range(n_chunks):
        p = slice(c * N_HEADS, (c + 1) * N_HEADS)
        res = _bdot(jnp.concatenate([w_t[p], rt[p]], axis=1), s_cur, _NT)
        u_c = res[:, :c_sz, :] + u_t[p]
        y_c = res[:, c_sz:, :] + _bdot(a_ra[p], u_c, _NN) + y_k[p]
        g_c = jnp.stack([g_end[c, :, h * HEAD_DIM:(h + 1) * HEAD_DIM] for h in range(N_HEADS)],
                        axis=0)
        s_cur = s_cur * g_c + _bdot(jnp.concatenate([u_c, vp[p]], axis=1),
                                    jnp.concatenate([ab[p], kb[p]], axis=1), _TN)
        y_rows.append(jnp.concatenate([y_c[h] for h in range(N_HEADS)], axis=1))
    state[...] = s_cur
    return jnp.concatenate(y_rows, axis=0)


def _mixer_kernel(x_ref, sh_ref, sc_ref, g_ref, w_in_ref, w_out_ref, cw_ref, vec_ref, mu_ref,
                  ln_ref, wgate_ref, wlora_ref, bd_ref, o_ref,
                  hbuf, xbbuf, zbuf, pc_last, lru_h, state, ymix):
    ts = x_ref.shape[1]
    s_idx = pl.program_id(1)

    @pl.when(s_idx == 0)
    def _reset():
        hbuf[pl.ds(0, CONF_HIST), :] = jnp.zeros((CONF_HIST, D_GRP), jnp.float32)
        xbbuf[pl.ds(0, SMALL_HIST), :] = jnp.zeros((SMALL_HIST, D_GRP), jnp.float32)
        zbuf[pl.ds(0, SMALL_HIST), :] = jnp.zeros((SMALL_HIST, D_GRP), jnp.float32)
        pc_last[...] = jnp.zeros(pc_last.shape, jnp.float32)
        lru_h[...] = jnp.zeros(lru_h.shape, jnp.float32)
        state[...] = jnp.zeros(state.shape, jnp.float32)

    x = x_ref[0]
    u = x * (1.0 + sc_ref[0]) + sh_ref[0]
    proj = jnp.dot(_bf16(u), w_in_ref[...], preferred_element_type=jnp.float32)

    vec = vec_ref[...]

    def vrow(i):
        return vec[i:i + 1, :]

    cw = cw_ref[...]

    val = proj[:, OFF_A:OFF_A + D_GRP]
    gate = proj[:, OFF_A + D_GRP:OFF_A + 2 * D_GRP]
    hbuf[pl.ds(CONF_HIST, ts), :] = val * jax.nn.sigmoid(gate)
    conv = _causal_conv(hbuf, CONF_HIST, ts, cw[CW_A:CW_A + CONF_KERNEL], CONF_KERNEL)
    conv = conv + vrow(V_CONV_A_BIAS)
    hbuf[pl.ds(0, CONF_HIST), :] = hbuf[pl.ds(ts, CONF_HIST), :]
    ln_a = _layer_norm(conv, vrow(V_LN_A_G), vrow(V_LN_A_B))
    ymix[:, 0:D_GRP] = ln_a * jax.nn.sigmoid(ln_a)

    xbbuf[pl.ds(SMALL_HIST, ts), :] = proj[:, OFF_B:OFF_B + D_GRP]
    gb = proj[:, OFF_B + D_GRP:OFF_B + 2 * D_GRP]
    ub = _causal_conv(xbbuf, SMALL_HIST, ts, cw[CW_B:CW_B + LRU_CONV], LRU_CONV)
    ub = ub + vrow(V_CONV_B_BIAS)
    xbbuf[pl.ds(0, SMALL_HIST), :] = xbbuf[pl.ds(ts, SMALL_HIST), :]
    gates = jnp.dot(_bf16(ub), wgate_ref[...], preferred_element_type=jnp.float32)
    r_gate = jax.nn.sigmoid(gates[:, :D_GRP] + vrow(V_B_RG))
    i_gate = jax.nn.sigmoid(gates[:, D_GRP:] + vrow(V_B_IG))
    log_a = (-LRU_C) * r_gate * _softplus(-vrow(V_LRU_LAMBDA))
    a_sc = jnp.exp(log_a)
    b_sc = jnp.sqrt(-jnp.tanh(log_a) * (a_sc * a_sc + 1.0)) * (i_gate * ub)
    d = 1
    while d < ts:
        a_sh = _shift_rows(a_sc, d, 1.0)
        b_sh = _shift_rows(b_sc, d, 0.0)
        b_sc = a_sc * b_sh + b_sc
        a_sc = a_sc * a_sh
        d *= 2
    h = b_sc + a_sc * lru_h[0:1, :]
    lru_h[0:1, :] = h[ts - 1:ts, :]
    ymix[:, D_GRP:2 * D_GRP] = h * jax.nn.gelu(gb, approximate=True)

    gbd = proj[:, OFF_D:OFF_D + D_GRP]
    zbuf[pl.ds(SMALL_HIST, ts), :] = (proj[:, OFF_D + D_GRP:OFF_D + 2 * D_GRP]
                                      * proj[:, OFF_D + 2 * D_GRP:OFF_D + 3 * D_GRP])
    convd = _causal_conv(zbuf, SMALL_HIST, ts, cw[CW_D:CW_D + SHORT_CONV], SHORT_CONV)
    zbuf[pl.ds(0, SMALL_HIST), :] = zbuf[pl.ds(ts, SMALL_HIST), :]
    ymix[:, 3 * D_GRP:4 * D_GRP] = gbd * convd

    pc = proj[:, OFF_C:OFF_C + C_COLS]
    prev = jnp.where(_row_iota(pc.shape) == 0, pc_last[0:1, :], pltpu.roll(pc, 1, axis=0))
    pc_last[0:1, :] = pc[ts - 1:ts, :]
    xs = pc + (prev - pc) * mu_ref[...]
    r = xs[:, 0:D_GRP]
    k = xs[:, D_GRP:2 * D_GRP]
    v = xs[:, 2 * D_GRP:3 * D_GRP]
    z = xs[:, 3 * D_GRP:C_COLS]
    lane = lax.broadcasted_iota(jnp.int32, z.shape, 1)
    z_act = jnp.where(lane < LORA_W, jnp.tanh(z),
                      jnp.where(lane < LORA_W + LORA_A, z, jax.nn.sigmoid(z)))
    lora = jnp.dot(_bf16(z_act), wlora_ref[...], preferred_element_type=jnp.float32)
    wlog = -_softplus(-(vrow(V_W0) + lora[:, 0:D_GRP])) - 0.5
    lw = -jnp.exp(wlog)
    a_icl = jax.nn.sigmoid(vrow(V_A0) + lora[:, D_GRP:2 * D_GRP])
    g_out = lora[:, 2 * D_GRP:3 * D_GRP]

    kk = k * vrow(V_K_K)
    kk = kk * lax.rsqrt(_head_sum(kk * kk, bd_ref) + 1e-12)
    kf = k * (1.0 + (a_icl - 1.0) * vrow(V_K_A))

    y = _rwkv_chunked(kk, r, -(kk * a_icl), kf, v, lw, state)
    inv_n = 1.0 / HEAD_DIM
    mu_y = _head_sum(y, bd_ref) * inv_n
    yc = y - mu_y
    var_y = _head_sum(yc * yc, bd_ref) * inv_n
    y = yc * lax.rsqrt(var_y + RWKV_GN_EPS) * vrow(V_GN_G) + vrow(V_GN_B)
    bonus = _head_sum(r * kf * vrow(V_R_K), bd_ref) * v
    ymix[:, 2 * D_GRP:3 * D_GRP] = (y + bonus) * g_out

    mix = jnp.dot(_bf16(ymix[...]), w_out_ref[...], preferred_element_type=jnp.float32)
    alpha_dn = ln_ref[2:3, :]
    res = alpha_dn * x + (1.0 + g_ref[0]) * mix
    o_ref[0] = _layer_norm(res, ln_ref[0:1, :], ln_ref[1:2, :])


def _token_mixer_layer(x, sh, sc, g, w_in, w_out, cw, vec, mu, ln, wgate, wlora, bd, ts):
    bsz, s, d = x.shape
    grid = (bsz, s // ts)
    full2 = lambda b, i: (0, 0)
    mod_spec = pl.BlockSpec((1, 1, d), lambda b, i: (b, 0, 0))
    f32 = jnp.float32

    def tile(n_cols):
        return pltpu.VMEM((ts, n_cols), f32)

    return pl.pallas_call(
        _mixer_kernel,
        grid=grid,
        in_specs=[
            pl.BlockSpec((1, ts, d), lambda b, i: (b, i, 0)),
            mod_spec, mod_spec, mod_spec,
            pl.BlockSpec(w_in.shape, full2),
            pl.BlockSpec(w_out.shape, full2),
            pl.BlockSpec(cw.shape, full2),
            pl.BlockSpec(vec.shape, full2),
            pl.BlockSpec(mu.shape, full2),
            pl.BlockSpec(ln.shape, full2),
            pl.BlockSpec(wgate.shape, full2),
            pl.BlockSpec(wlora.shape, full2),
            pl.BlockSpec(bd.shape, full2),
        ],
        out_specs=pl.BlockSpec((1, ts, d), lambda b, i: (b, i, 0)),
        out_shape=jax.ShapeDtypeStruct((bsz, s, d), f32),
        scratch_shapes=[
            pltpu.VMEM((CONF_HIST + ts, D_GRP), f32),
            pltpu.VMEM((SMALL_HIST + ts, D_GRP), f32),
            pltpu.VMEM((SMALL_HIST + ts, D_GRP), f32),
            pltpu.VMEM((SUBLANES, C_COLS), f32),
            pltpu.VMEM((SUBLANES, D_GRP), f32),
            pltpu.VMEM((N_HEADS, HEAD_DIM, HEAD_DIM), f32),
            tile(D_MODEL),
        ],
        compiler_params=pltpu.CompilerParams(
            dimension_semantics=("arbitrary", "arbitrary"),
            vmem_limit_bytes=VMEM_LIMIT_BYTES),
        name="token_mixers",
    )(x, sh, sc, g, w_in, w_out, cw, vec, mu, ln, wgate, wlora, bd)


def _first_index(mask, idx, sentinel):
    return jnp.min(jnp.where(mask, idx, sentinel), axis=0, keepdims=True)


def _router_kernel(x_ref, sh_ref, sc_ref, wr_ref, eb_ref, su_ref, sl_ref,
                   slot_ref, wt_ref, plan_ref):
    x = x_ref[0]
    u = x * (1.0 + sc_ref[0]) + sh_ref[0]
    tm = u.shape[0]
    logits = _dot_f32(wr_ref[...], u, ((1,), (1,)))
    scores = jax.nn.sigmoid(logits)
    biased = scores + eb_ref[...]
    neg_inf = jnp.float32(-jnp.inf)
    sub = lax.broadcasted_iota(jnp.int32, (E_PER_GROUP, tm), 0).astype(jnp.float32)

    groups = [biased[g * E_PER_GROUP:(g + 1) * E_PER_GROUP, :] for g in range(N_GROUPS)]
    gscore = []
    for blk in groups:
        m1 = jnp.max(blk, axis=0, keepdims=True)
        first = _first_index(blk == m1, sub, float(E_PER_GROUP))
        m2 = jnp.max(jnp.where(sub == first, neg_inf, blk), axis=0, keepdims=True)
        gscore.append(m1 + m2)
    gs = jnp.concatenate(gscore, axis=0)
    gidx = lax.broadcasted_iota(jnp.int32, (N_GROUPS, tm), 0).astype(jnp.float32)
    keep = jnp.zeros((N_GROUPS, tm), jnp.float32)
    for _ in range(TOPK_GROUPS):
        m = jnp.max(gs, axis=0, keepdims=True)
        first = _first_index(gs == m, gidx, float(N_GROUPS))
        sel = gidx == first
        keep = jnp.where(sel, 1.0, keep)
        gs = jnp.where(sel, neg_inf, gs)

    masked = [jnp.where(keep[g:g + 1, :] > 0.5, groups[g], neg_inf) for g in range(N_GROUPS)]
    eidx = [sub + float(g * E_PER_GROUP) for g in range(N_GROUPS)]
    chosen = [jnp.zeros((E_PER_GROUP, tm), jnp.float32) for _ in range(N_GROUPS)]
    for _ in range(TOP_K):
        m = functools.reduce(jnp.maximum, [jnp.max(b, axis=0, keepdims=True) for b in masked])
        first = functools.reduce(
            jnp.minimum,
            [_first_index(masked[g] == m, eidx[g], float(N_EXPERTS)) for g in range(N_GROUPS)])
        for g in range(N_GROUPS):
            sel = eidx[g] == first
            chosen[g] = jnp.where(sel, 1.0, chosen[g])
            masked[g] = jnp.where(sel, neg_inf, masked[g])

    picked = [jnp.where(chosen[g] > 0.5, scores[g * E_PER_GROUP:(g + 1) * E_PER_GROUP, :], 0.0)
              for g in range(N_GROUPS)]
    total = functools.reduce(jnp.add, [jnp.sum(p, axis=0, keepdims=True) for p in picked])
    wts = jnp.concatenate([p / total * ROUTED_SCALE for p in picked], axis=0)

    sel = jnp.concatenate(chosen, axis=0)
    sel_bf = _bf16(sel)
    rank = jnp.dot(sel_bf, su_ref[...], preferred_element_type=jnp.float32)
    cnt = jnp.sum(sel, axis=1, keepdims=True)
    off = _dot_f32(sl_ref[...].astype(jnp.float32), jnp.broadcast_to(cnt, (N_EXPERTS, LANES)))
    slot = off[:, 0:1] + rank
    kidx = jnp.dot(sl_ref[...], sel_bf, preferred_element_type=jnp.float32)
    slot_rows, wt_rows = [], []
    for kk in range(TOP_K):
        pick = (sel > 0.5) & (kidx == float(kk))
        slot_rows.append(jnp.sum(jnp.where(pick, slot, 0.0), axis=0, keepdims=True))
        wt_rows.append(jnp.sum(jnp.where(pick, wts, 0.0), axis=0, keepdims=True))
    lane_pad = [jnp.zeros((LANES - TOP_K, tm), jnp.float32)]
    slot_ref[0] = (jnp.concatenate(slot_rows + lane_pad, axis=0).T
                   * float(SLAB_ROWS)).astype(jnp.int32)
    wt_ref[0] = jnp.concatenate(wt_rows + lane_pad, axis=0).T
    e_sub = lax.broadcasted_iota(jnp.int32, (N_EXPERTS, LANES), 0)
    e_lane = lax.broadcasted_iota(jnp.int32, (N_EXPERTS, LANES), 1)
    diag = e_sub == e_lane
    off_row = jnp.sum(jnp.where(diag, off, 0.0), axis=0, keepdims=True)
    cnt_row = jnp.sum(jnp.where(diag, jnp.broadcast_to(cnt, (N_EXPERTS, LANES)), 0.0),
                      axis=0, keepdims=True)
    plan_ref[0] = jnp.concatenate([off_row, cnt_row], axis=0).astype(jnp.int32)


def _router_layer(x, sh, sc, w_router_t, e_bias_col, su, sl, tm):
    bsz, s, d = x.shape
    per_b = s // tm
    n_tiles = bsz * per_b
    mod_spec = pl.BlockSpec((1, 1, d), lambda b, i: (b, 0, 0))
    const2 = lambda b, i: (0, 0)
    tile3 = lambda b, i: (b * per_b + i, 0, 0)
    return pl.pallas_call(
        _router_kernel,
        grid=(bsz, per_b),
        in_specs=[
            pl.BlockSpec((1, tm, d), lambda b, i: (b, i, 0)),
            mod_spec, mod_spec,
            pl.BlockSpec(w_router_t.shape, const2),
            pl.BlockSpec(e_bias_col.shape, const2),
            pl.BlockSpec(su.shape, const2),
            pl.BlockSpec(sl.shape, const2),
        ],
        out_specs=[
            pl.BlockSpec((1, tm, LANES), tile3),
            pl.BlockSpec((1, tm, LANES), tile3),
            pl.BlockSpec((1, 2, LANES), tile3),
        ],
        out_shape=[
            jax.ShapeDtypeStruct((n_tiles, tm, LANES), jnp.int32),
            jax.ShapeDtypeStruct((n_tiles, tm, LANES), jnp.float32),
            jax.ShapeDtypeStruct((n_tiles, 2, LANES), jnp.int32),
        ],
        compiler_params=pltpu.CompilerParams(
            dimension_semantics=("arbitrary", "arbitrary"),
            vmem_limit_bytes=VMEM_LIMIT_BYTES),
        name="moe_router",
    )(x, sh, sc, w_router_t, e_bias_col, su, sl)


def _swiglu(u_bf16, w13, w2):
    hcat = jnp.dot(u_bf16, w13, preferred_element_type=jnp.float32)
    half = hcat.shape[1] // 2
    gate, up = hcat[:, :half], hcat[:, half:]
    act = gate * jax.nn.sigmoid(gate) * up
    return jnp.dot(_bf16(act), w2, preferred_element_type=jnp.float32)


def _weight_copies(w13_hbm, w2_hbm, w13_buf, w2_buf, sems, e, slot):
    return (pltpu.make_async_copy(w13_hbm.at[e], w13_buf.at[slot], sems.at[0, slot]),
            pltpu.make_async_copy(w2_hbm.at[e], w2_buf.at[slot], sems.at[1, slot]))


def _start_weight_copies(w13_hbm, w2_hbm, w13_buf, w2_buf, sems, e, slot):
    for queue, cp in enumerate(_weight_copies(w13_hbm, w2_hbm, w13_buf, w2_buf, sems, e, slot)):
        cp.start(priority=queue)


def _expert_kernel(plan_ref, slot_ref, wt_ref, x_ref, sh_ref, sc_ref, g_ref, w13_hbm, w2_hbm,
                   ws13_ref, ws2_ref, ln_ref, o_ref, rows_f, xf, w13_buf, w2_buf, sems):
    i = pl.program_id(0)
    tm = x_ref.shape[0]
    n_exp = w13_hbm.shape[0]
    n_col = x_ref.shape[1] // LANES

    @pl.when(i == 0)
    def _clear():
        xf[...] = jnp.zeros(xf.shape, xf.dtype)

    for e0 in range(W_AHEAD):
        _start_weight_copies(w13_hbm, w2_hbm, w13_buf, w2_buf, sems, e0, e0)

    u = x_ref[...] * (1.0 + sc_ref[0]) + sh_ref[0]
    for j in range(n_col):
        rows_f[pl.ds(j, tm, stride=SLAB_ROWS), :] = u[:, j * LANES:(j + 1) * LANES]

    def scatter(t, carry):
        row = rows_f[pl.ds(pl.multiple_of(t * SLAB_ROWS, SLAB_ROWS), SLAB_ROWS), :]
        for k in range(TOP_K):
            dst = pl.multiple_of(slot_ref[t * TOP_K + k], SLAB_ROWS)
            xf[pl.ds(dst, SLAB_ROWS), :] = row
        return carry

    lax.fori_loop(0, tm, scatter, 0)

    def expert(e, carry):
        slot = lax.rem(e, W_SLOTS)
        for cp in _weight_copies(w13_hbm, w2_hbm, w13_buf, w2_buf, sems, e, slot):
            cp.wait()

        @pl.when(e + W_AHEAD < n_exp)
        def _prefetch():
            nxt = e + W_AHEAD
            _start_weight_copies(w13_hbm, w2_hbm, w13_buf, w2_buf, sems, nxt, lax.rem(nxt, W_SLOTS))

        first = plan_ref[i * 2 * LANES + e]
        count = plan_ref[i * 2 * LANES + LANES + e]
        n_blocks = lax.div(count + (EXPERT_ROWS - 1), EXPERT_ROWS)

        def block(j, c2):
            base = (first + j * EXPERT_ROWS) * SLAB_ROWS
            old = [xf[pl.ds(base + q, EXPERT_ROWS, stride=SLAB_ROWS), :] for q in range(n_col)]
            xb = jnp.concatenate([_bf16(o) for o in old], axis=1)
            y = _swiglu(xb, w13_buf[slot], w2_buf[slot])
            live = (lax.broadcasted_iota(jnp.int32, (EXPERT_ROWS, LANES), 0)
                    < count - j * EXPERT_ROWS)
            for q in range(n_col):
                xf[pl.ds(base + q, EXPERT_ROWS, stride=SLAB_ROWS), :] = jnp.where(
                    live, y[:, q * LANES:(q + 1) * LANES], old[q])
            return c2

        lax.fori_loop(0, n_blocks, block, 0)
        return carry

    lax.fori_loop(0, n_exp, expert, 0)

    def gather(t, carry):
        acc = jnp.zeros((SLAB_ROWS, LANES), jnp.float32)
        for k in range(TOP_K):
            src = pl.multiple_of(slot_ref[t * TOP_K + k], SLAB_ROWS)
            acc = acc + wt_ref[t * TOP_K + k] * xf[pl.ds(src, SLAB_ROWS), :]
        rows_f[pl.ds(pl.multiple_of(t * SLAB_ROWS, SLAB_ROWS), SLAB_ROWS), :] = acc
        return carry

    lax.fori_loop(0, tm, gather, 0)
    routed = jnp.concatenate([rows_f[pl.ds(j, tm, stride=SLAB_ROWS), :] for j in range(n_col)],
                             axis=1)
    x = x_ref[...]
    u = x * (1.0 + sc_ref[0]) + sh_ref[0]
    ffn = routed + _swiglu(_bf16(u), ws13_ref[...], ws2_ref[...])
    res = ln_ref[2:3, :] * x + (1.0 + g_ref[0]) * ffn
    o_ref[...] = _layer_norm(res, ln_ref[0:1, :], ln_ref[1:2, :])


def _expert_layer(x, sh, sc, g, plan, slots, wts, w13, w2, ws13, ws2, ln, tm):
    bsz, s, d = x.shape
    per_b = s // tm
    n_tiles = bsz * per_b
    mod_spec = pl.BlockSpec((1, 1, d), lambda i, plan: (i // per_b, 0, 0))
    full2 = lambda i, plan: (0, 0)
    once = pl.Buffered(1)
    tile_smem = pl.BlockSpec((TOP_K * tm,), lambda i, plan: (i,), memory_space=pltpu.SMEM)
    slab_rows = (TOP_K * tm + EXPERT_ROWS) * SLAB_ROWS
    grid_spec = pltpu.PrefetchScalarGridSpec(
        num_scalar_prefetch=1,
        grid=(n_tiles,),
        in_specs=[
            tile_smem, tile_smem,
            pl.BlockSpec((tm, d), lambda i, plan: (i, 0), pipeline_mode=once),
            mod_spec, mod_spec, mod_spec,
            pl.BlockSpec(memory_space=pl.ANY),
            pl.BlockSpec(memory_space=pl.ANY),
            pl.BlockSpec(ws13.shape, full2, pipeline_mode=once),
            pl.BlockSpec(ws2.shape, full2, pipeline_mode=once),
            pl.BlockSpec(ln.shape, full2),
        ],
        out_specs=pl.BlockSpec((tm, d), lambda i, plan: (i, 0), pipeline_mode=once),
        scratch_shapes=[
            pltpu.VMEM((tm * SLAB_ROWS, LANES), jnp.float32),
            pltpu.VMEM((slab_rows, LANES), jnp.float32),
            pltpu.VMEM((W_SLOTS,) + w13.shape[1:], w13.dtype),
            pltpu.VMEM((W_SLOTS,) + w2.shape[1:], w2.dtype),
            pltpu.SemaphoreType.DMA((2, W_SLOTS)),
        ],
    )
    out = pl.pallas_call(
        _expert_kernel,
        grid_spec=grid_spec,
        out_shape=jax.ShapeDtypeStruct((bsz * s, d), jnp.float32),
        compiler_params=pltpu.CompilerParams(
            dimension_semantics=("arbitrary",),
            vmem_limit_bytes=VMEM_LIMIT_BYTES),
        name="moe_experts",
    )(plan.reshape(-1), slots[:, :, :TOP_K].reshape(-1), wts[:, :, :TOP_K].reshape(-1),
      x.reshape(bsz * s, d), sh, sc, g,
      w13, w2, ws13, ws2, ln)
    return out.reshape(bsz, s, d)


def _block_diag(w):
    h, n, _ = w.shape
    eye = jnp.eye(h, dtype=w.dtype)
    return (eye[:, None, :, None] * w[:, :, None, :]).reshape(h * n, h * n)


def _pad_rows(a, n_rows):
    return jnp.concatenate([a, jnp.zeros((n_rows - a.shape[0], a.shape[1]), a.dtype)], axis=0)


def _tile_rows(seq_len, want):
    t = min(want, seq_len)
    assert seq_len % t == 0 and t % RWKV_CHUNK == 0
    return t


def kernel(x, c, w_mod, b_mod, w_in, w_out, conv_a, conv_a_bias, ln_a_g, ln_a_b, conv_b, conv_b_bias, w_rg, b_rg, w_ig, b_ig, lru_lambda, mu_c, w0, w_w2, a0, w_a2, w_g2, k_k, k_a, r_k, gn_g, gn_b, conv_d, ln1_g, ln1_b, w_router, e_bias, w13, w2, ws13, ws2, ln2_g, ln2_b):
    n_layers = w_mod.shape[0]
    bsz, s, d = x.shape
    alpha_dn = (2.0 * n_layers) ** 0.25
    ts_mix = _tile_rows(s, 512)
    tm_moe = _tile_rows(s, 1024)
    bf = jnp.bfloat16

    mod = _modulation(c, w_mod, b_mod)
    bd = _block_diag(jnp.ones((N_HEADS, HEAD_DIM, HEAD_DIM), bf))
    alpha_row = jnp.full((1, d), alpha_dn, jnp.float32)
    tri_tok = jnp.triu(jnp.ones((tm_moe, tm_moe), bf), k=1)
    tri_exp = jnp.tril(jnp.ones((N_EXPERTS, N_EXPERTS), bf), k=-1)

    for l in range(n_layers):
        sh1, sc1, g1, sh2, sc2, g2 = [mod[l, :, i * d:(i + 1) * d].reshape(bsz, 1, d)
                                      for i in range(6)]
        cw = _pad_rows(jnp.concatenate([conv_a[l], conv_b[l], conv_d[l]], axis=0), N_CW_ROWS)
        vec = _pad_rows(jnp.stack([
            conv_a_bias[l], ln_a_g[l], ln_a_b[l], conv_b_bias[l], b_rg[l], b_ig[l], lru_lambda[l],
            w0[l], a0[l], k_k[l], k_a[l], r_k[l].reshape(D_GRP), gn_g[l], gn_b[l]], axis=0),
            N_VEC_ROWS)
        wgate = jnp.concatenate([_block_diag(w_rg[l]), _block_diag(w_ig[l])], axis=1).astype(bf)
        wlora = jnp.zeros((LORA_W + LORA_A + LORA_G, 3 * D_GRP), jnp.float32)
        wlora = wlora.at[0:LORA_W, 0:D_GRP].set(w_w2[l])
        wlora = wlora.at[LORA_W:LORA_W + LORA_A, D_GRP:2 * D_GRP].set(w_a2[l])
        wlora = wlora.at[LORA_W + LORA_A:, 2 * D_GRP:].set(w_g2[l]).astype(bf)
        ln1 = jnp.concatenate([ln1_g[l][None], ln1_b[l][None], alpha_row], axis=0)
        ln2 = jnp.concatenate([ln2_g[l][None], ln2_b[l][None], alpha_row], axis=0)

        x = _token_mixer_layer(x, sh1, sc1, g1, w_in[l].astype(bf), w_out[l].astype(bf), cw, vec,
                               mu_c[l][None], ln1, wgate, wlora, bd, ts_mix)
        slots, wts, plan = _router_layer(x, sh2, sc2, w_router[l].T, e_bias[l][:, None],
                                         tri_tok, tri_exp, tm_moe)
        x = _expert_layer(x, sh2, sc2, g2, plan, slots, wts, w13[l].astype(bf), w2[l].astype(bf),
                          ws13[l].astype(bf), ws2[l].astype(bf), ln2, tm_moe)
    return x
```

```python
import functools

import jax
import jax.numpy as jnp
from jax import lax
from jax.experimental import pallas as pl
from jax.experimental.pallas import tpu as pltpu

D_MODEL = 1024
HEAD_DIM = 64
D_GRP = 256
N_HEADS = D_GRP // HEAD_DIM
CONF_KERNEL = 31
LRU_CONV = 4
LRU_C = 8.0
SHORT_CONV = 3
LORA_W, LORA_A, LORA_G = 32, 32, 64
C_COLS = 3 * D_GRP + LORA_W + LORA_A + LORA_G
P_IN = 4 * D_GRP + C_COLS + 3 * D_GRP
RWKV_GN_EPS = 64e-5
N_EXPERTS = 64
TOP_K = 8
N_GROUPS = 8
TOPK_GROUPS = 4
E_PER_GROUP = N_EXPERTS // N_GROUPS
D_EXPERT = 256
ROUTED_SCALE = 2.5
LN_EPS = 1e-5

OFF_A = 0
OFF_B = 2 * D_GRP
OFF_C = 4 * D_GRP
OFF_D = OFF_C + C_COLS

SUBLANES = 8
LANES = 128
VMEM_LIMIT_BYTES = 58 * 1024 * 1024

SLAB_ROWS = D_MODEL // LANES
EXPERT_ROWS = 144
W_SLOTS = 4
W_AHEAD = W_SLOTS - 1
RWKV_CHUNK = 64
CONF_HIST = 32
SMALL_HIST = 8

(V_CONV_A_BIAS, V_LN_A_G, V_LN_A_B, V_CONV_B_BIAS, V_B_RG, V_B_IG, V_LRU_LAMBDA, V_W0, V_A0,
 V_K_K, V_K_A, V_R_K, V_GN_G, V_GN_B) = range(14)
N_VEC_ROWS = 16
CW_A, CW_B, CW_D = 0, CONF_KERNEL, CONF_KERNEL + LRU_CONV
N_CW_ROWS = 40


def _bf16(x):
    return x.astype(jnp.bfloat16)


def _dot(a, b, dims=((1,), (0,))):
    return lax.dot_general(_bf16(a), _bf16(b), (dims, ((), ())),
                           preferred_element_type=jnp.float32)


def _split(a):
    hi = _bf16(a)
    lo = _bf16(a - hi.astype(jnp.float32))
    return hi, lo


def _dot_f32(a, b, dims=((1,), (0,))):
    a_hi, a_lo = _split(a)
    b_hi, b_lo = _split(b)
    dn = (dims, ((), ()))
    f = functools.partial(lax.dot_general, dimension_numbers=dn,
                          preferred_element_type=jnp.float32)
    return f(a_hi, b_hi) + (f(a_hi, b_lo) + f(a_lo, b_hi))


def _dot_lhs_f32(a, b_bf16):
    a_hi, a_lo = _split(a)
    f = functools.partial(jnp.dot, preferred_element_type=jnp.float32)
    return f(a_hi, b_bf16) + f(a_lo, b_bf16)


def _layer_norm(x, g, b):
    mu = jnp.mean(x, axis=-1, keepdims=True)
    xc = x - mu
    var = jnp.mean(xc * xc, axis=-1, keepdims=True)
    return xc * lax.rsqrt(var + LN_EPS) * g + b


def _softplus(x):
    return jnp.maximum(x, 0.0) + jnp.log1p(jnp.exp(-jnp.abs(x)))


def _row_iota(shape):
    return lax.broadcasted_iota(jnp.int32, shape, 0)


def _shift_rows(x, d, fill):
    rolled = pltpu.roll(x, d, axis=0)
    return jnp.where(_row_iota(x.shape) >= d, rolled, fill)


def _mod_kernel(c_ref, w_ref, b_ref, o_ref):
    c = c_ref[...]
    c_act = c * jax.nn.sigmoid(c)
    o_ref[0] = _dot_f32(c_act, w_ref[0]) + b_ref[0]


def _modulation(c, w_mod, b_mod):
    n_layers, d, d6 = w_mod.shape
    bsz = c.shape[0]
    tn = D_MODEL
    return pl.pallas_call(
        _mod_kernel,
        grid=(n_layers, d6 // tn),
        in_specs=[
            pl.BlockSpec((bsz, d), lambda l, j: (0, 0)),
            pl.BlockSpec((1, d, tn), lambda l, j: (l, 0, j)),
            pl.BlockSpec((1, 1, tn), lambda l, j: (l, 0, j)),
        ],
        out_specs=pl.BlockSpec((1, bsz, tn), lambda l, j: (l, 0, j)),
        out_shape=jax.ShapeDtypeStruct((n_layers, bsz, d6), jnp.float32),
        compiler_params=pltpu.CompilerParams(
            dimension_semantics=("arbitrary", "arbitrary"),
            vmem_limit_bytes=VMEM_LIMIT_BYTES),
        name="adaln_modulation",
    )(c, w_mod, b_mod.reshape(n_layers, 1, d6))


def _causal_conv(buf_ref, hist, ts, w_rows, n_taps):
    acc = None
    for k in range(n_taps):
        off = hist - (n_taps - 1) + k
        term = w_rows[k:k + 1, :] * buf_ref[pl.ds(off, ts), :]
        acc = term if acc is None else acc + term
    return acc


def _causal_conv_long(buf_ref, hist, ts, w_rows, n_taps):
    ext = ts + SUBLANES
    out = None
    for r in range(SUBLANES):
        part = None
        for q in range((n_taps - 1 - r) // SUBLANES + 1):
            lag = SUBLANES * q + r
            term = (w_rows[n_taps - 1 - lag:n_taps - lag, :]
                    * buf_ref[pl.ds(hist - SUBLANES - SUBLANES * q, ext), :])
            part = term if part is None else part + term
        shifted = part if r == 0 else pltpu.roll(part, r, axis=0)
        piece = shifted[SUBLANES:, :]
        out = piece if out is None else out + piece
    return out


def _head_sum(x, bd_ref):
    return _dot_lhs_f32(x, bd_ref[...])


_NN = ((2,), (1,))
_NT = ((2,), (2,))
_TN = ((1,), (1,))


def _bdot(a, b, dims):
    return lax.dot_general(_bf16(a), _bf16(b), (dims, ((0,), (0,))),
                           preferred_element_type=jnp.float32)


def _to_problems(x):
    n_chunks = x.shape[0] // RWKV_CHUNK
    return jnp.stack([x[c * RWKV_CHUNK:(c + 1) * RWKV_CHUNK, h * HEAD_DIM:(h + 1) * HEAD_DIM]
                      for c in range(n_chunks) for h in range(N_HEADS)], axis=0)


def _rwkv_chunked(kk, r, alpha, kf, v, lw, state):
    ts = kk.shape[0]
    n_chunks = ts // RWKV_CHUNK
    cl = lw
    seg_row = _row_iota(cl.shape) % RWKV_CHUNK
    d = 1
    while d < RWKV_CHUNK:
        cl = cl + jnp.where(seg_row >= d, pltpu.roll(cl, d, axis=0), 0.0)
        d *= 2
    cl_end = cl.reshape(n_chunks, RWKV_CHUNK, D_GRP)[:, RWKV_CHUNK - 1:RWKV_CHUNK, :]
    cl_end_rows = jnp.broadcast_to(cl_end, (n_chunks, RWKV_CHUNK, D_GRP)).reshape(ts, D_GRP)
    g_inv = jnp.exp(-cl)
    tail = jnp.exp(cl_end_rows - cl)
    g_end = jnp.exp(cl_end)
    bt = _to_problems(_bf16(kk * jnp.exp(cl - lw)))
    rt = _to_problems(_bf16(r * jnp.exp(cl)))
    at = _to_problems(_bf16(alpha * g_inv))
    kt = _to_problems(_bf16(kf * g_inv))
    ab = _to_problems(_bf16(alpha * tail))
    kb = _to_problems(_bf16(kf * tail))
    vp = _to_problems(_bf16(v))

    c_sz = RWKV_CHUNK
    tri_r = lax.broadcasted_iota(jnp.int32, (1, c_sz, c_sz), 1)
    tri_c = lax.broadcasted_iota(jnp.int32, (1, c_sz, c_sz), 2)
    strict = tri_r > tri_c
    incl = tri_r >= tri_c
    eye = (tri_r == tri_c).astype(jnp.float32)

    amat = _bdot(jnp.concatenate([bt, rt], axis=1), jnp.concatenate([at, kt], axis=1), _NT)
    a_ba = jnp.where(strict, amat[:, :c_sz, :c_sz], 0.0)
    a_bk = jnp.where(strict, amat[:, :c_sz, c_sz:], 0.0)
    a_ra = jnp.where(incl, amat[:, c_sz:, :c_sz], 0.0)
    a_rk = jnp.where(incl, amat[:, c_sz:, c_sz:], 0.0)
    tinv = eye + a_ba
    pw = a_ba
    step = 2
    while step < c_sz:
        pw = _bdot(pw, pw, _NN)
        tinv = tinv + _bdot(pw, tinv, _NN)
        step *= 2
    tx = _bdot(tinv, jnp.concatenate([bt, _bf16(_bdot(a_bk, vp, _NN))], axis=2), _NN)
    w_t, u_t = tx[:, :, :HEAD_DIM], tx[:, :, HEAD_DIM:]
    y_k = _bdot(a_rk, vp, _NN)

    s_cur = state[...]
    y_rows = []
    for c in range(n_chunks):
        p = slice(c * N_HEADS, (c + 1) * N_HEADS)
        res = _bdot(jnp.concatenate([_bf16(w_t[p]), rt[p]], axis=1), s_cur, _NT)
        u_c = res[:, :c_sz, :] + u_t[p]
        y_c = res[:, c_sz:, :] + _bdot(a_ra[p], u_c, _NN) + y_k[p]
        g_c = jnp.stack([g_end[c, :, h * HEAD_DIM:(h + 1) * HEAD_DIM] for h in range(N_HEADS)],
                        axis=0)
        s_cur = s_cur * g_c + _bdot(jnp.concatenate([_bf16(u_c), vp[p]], axis=1),
                                    jnp.concatenate([ab[p], kb[p]], axis=1), _TN)
        y_rows.append(jnp.concatenate([y_c[h] for h in range(N_HEADS)], axis=1))
    state[...] = s_cur
    return jnp.concatenate(y_rows, axis=0)


def _mixer_kernel(x_ref, sh_ref, sc_ref, g_ref, w_in_ref, w_out_ref, cw_ref, vec_ref, mu_ref,
                  ln_ref, wgate_ref, wlora_ref, bd_ref, o_ref,
                  hbuf, xbbuf, zbuf, pc_last, lru_h, state, ymix):
    ts = x_ref.shape[1]
    s_idx = pl.program_id(1)

    @pl.when(s_idx == 0)
    def _reset():
        hbuf[pl.ds(0, CONF_HIST), :] = jnp.zeros((CONF_HIST, D_GRP), jnp.float32)
        xbbuf[pl.ds(0, SMALL_HIST), :] = jnp.zeros((SMALL_HIST, D_GRP), jnp.float32)
        zbuf[pl.ds(0, SMALL_HIST), :] = jnp.zeros((SMALL_HIST, D_GRP), jnp.float32)
        pc_last[...] = jnp.zeros(pc_last.shape, jnp.float32)
        lru_h[...] = jnp.zeros(lru_h.shape, jnp.float32)
        state[...] = jnp.zeros(state.shape, jnp.float32)

    x = x_ref[0]
    u = x * (1.0 + sc_ref[0]) + sh_ref[0]
    proj = jnp.dot(_bf16(u), w_in_ref[...], preferred_element_type=jnp.float32)

    vec = vec_ref[...]

    def vrow(i):
        return vec[i:i + 1, :]

    cw = cw_ref[...]

    val = proj[:, OFF_A:OFF_A + D_GRP]
    gate = proj[:, OFF_A + D_GRP:OFF_A + 2 * D_GRP]
    hbuf[pl.ds(CONF_HIST, ts), :] = val * jax.nn.sigmoid(gate)
    conv = _causal_conv_long(hbuf, CONF_HIST, ts, cw[CW_A:CW_A + CONF_KERNEL], CONF_KERNEL)
    conv = conv + vrow(V_CONV_A_BIAS)
    hbuf[pl.ds(0, CONF_HIST), :] = hbuf[pl.ds(ts, CONF_HIST), :]
    ln_a = _layer_norm(conv, vrow(V_LN_A_G), vrow(V_LN_A_B))
    ymix[:, 0:D_GRP] = ln_a * jax.nn.sigmoid(ln_a)

    xbbuf[pl.ds(SMALL_HIST, ts), :] = proj[:, OFF_B:OFF_B + D_GRP]
    gb = proj[:, OFF_B + D_GRP:OFF_B + 2 * D_GRP]
    ub = _causal_conv(xbbuf, SMALL_HIST, ts, cw[CW_B:CW_B + LRU_CONV], LRU_CONV)
    ub = ub + vrow(V_CONV_B_BIAS)
    xbbuf[pl.ds(0, SMALL_HIST), :] = xbbuf[pl.ds(ts, SMALL_HIST), :]
    gates = jnp.dot(_bf16(ub), wgate_ref[...], preferred_element_type=jnp.float32)
    r_gate = jax.nn.sigmoid(gates[:, :D_GRP] + vrow(V_B_RG))
    i_gate = jax.nn.sigmoid(gates[:, D_GRP:] + vrow(V_B_IG))
    log_a = (-LRU_C) * r_gate * _softplus(-vrow(V_LRU_LAMBDA))
    a_sc = jnp.exp(log_a)
    b_sc = jnp.sqrt(-jnp.tanh(log_a) * (a_sc * a_sc + 1.0)) * (i_gate * ub)
    d = 1
    while d < ts:
        a_sh = _shift_rows(a_sc, d, 1.0)
        b_sh = _shift_rows(b_sc, d, 0.0)
        b_sc = a_sc * b_sh + b_sc
        a_sc = a_sc * a_sh
        d *= 2
    h = b_sc + a_sc * lru_h[0:1, :]
    lru_h[0:1, :] = h[ts - 1:ts, :]
    ymix[:, D_GRP:2 * D_GRP] = h * jax.nn.gelu(gb, approximate=True)

    gbd = proj[:, OFF_D:OFF_D + D_GRP]
    zbuf[pl.ds(SMALL_HIST, ts), :] = (proj[:, OFF_D + D_GRP:OFF_D + 2 * D_GRP]
                                      * proj[:, OFF_D + 2 * D_GRP:OFF_D + 3 * D_GRP])
    convd = _causal_conv(zbuf, SMALL_HIST, ts, cw[CW_D:CW_D + SHORT_CONV], SHORT_CONV)
    zbuf[pl.ds(0, SMALL_HIST), :] = zbuf[pl.ds(ts, SMALL_HIST), :]
    ymix[:, 3 * D_GRP:4 * D_GRP] = gbd * convd

    pc = proj[:, OFF_C:OFF_C + C_COLS]
    prev = jnp.where(_row_iota(pc.shape) == 0, pc_last[0:1, :], pltpu.roll(pc, 1, axis=0))
    pc_last[0:1, :] = pc[ts - 1:ts, :]
    xs = pc + (prev - pc) * mu_ref[...]
    r = xs[:, 0:D_GRP]
    k = xs[:, D_GRP:2 * D_GRP]
    v = xs[:, 2 * D_GRP:3 * D_GRP]
    z = xs[:, 3 * D_GRP:C_COLS]
    lane = lax.broadcasted_iota(jnp.int32, z.shape, 1)
    z_act = jnp.where(lane < LORA_W, jnp.tanh(z),
                      jnp.where(lane < LORA_W + LORA_A, z, jax.nn.sigmoid(z)))
    lora = jnp.dot(_bf16(z_act), wlora_ref[...], preferred_element_type=jnp.float32)
    wlog = -_softplus(-(vrow(V_W0) + lora[:, 0:D_GRP])) - 0.5
    lw = -jnp.exp(wlog)
    a_icl = jax.nn.sigmoid(vrow(V_A0) + lora[:, D_GRP:2 * D_GRP])
    g_out = lora[:, 2 * D_GRP:3 * D_GRP]

    kk = k * vrow(V_K_K)
    kk = kk * lax.rsqrt(_head_sum(kk * kk, bd_ref) + 1e-12)
    kf = k * (1.0 + (a_icl - 1.0) * vrow(V_K_A))

    y = _rwkv_chunked(kk, r, -(kk * a_icl), kf, v, lw, state)
    inv_n = 1.0 / HEAD_DIM
    mu_y = _head_sum(y, bd_ref) * inv_n
    yc = y - mu_y
    var_y = _head_sum(yc * yc, bd_ref) * inv_n
    y = yc * lax.rsqrt(var_y + RWKV_GN_EPS) * vrow(V_GN_G) + vrow(V_GN_B)
    bonus = _head_sum(r * kf * vrow(V_R_K), bd_ref) * v
    ymix[:, 2 * D_GRP:3 * D_GRP] = (y + bonus) * g_out

    mix = jnp.dot(_bf16(ymix[...]), w_out_ref[...], preferred_element_type=jnp.float32)
    alpha_dn = ln_ref[2:3, :]
    res = alpha_dn * x + (1.0 + g_ref[0]) * mix
    o_ref[0] = _layer_norm(res, ln_ref[0:1, :], ln_ref[1:2, :])


def _token_mixer_layer(x, sh, sc, g, w_in, w_out, cw, vec, mu, ln, wgate, wlora, bd, ts):
    bsz, s, d = x.shape
    grid = (bsz, s // ts)
    full2 = lambda b, i: (0, 0)
    mod_spec = pl.BlockSpec((1, 1, d), lambda b, i: (b, 0, 0))
    f32 = jnp.float32

    def tile(n_cols):
        return pltpu.VMEM((ts, n_cols), f32)

    return pl.pallas_call(
        _mixer_kernel,
        grid=grid,
        in_specs=[
            pl.BlockSpec((1, ts, d), lambda b, i: (b, i, 0)),
            mod_spec, mod_spec, mod_spec,
            pl.BlockSpec(w_in.shape, full2),
            pl.BlockSpec(w_out.shape, full2),
            pl.BlockSpec(cw.shape, full2),
            pl.BlockSpec(vec.shape, full2),
            pl.BlockSpec(mu.shape, full2),
            pl.BlockSpec(ln.shape, full2),
            pl.BlockSpec(wgate.shape, full2),
            pl.BlockSpec(wlora.shape, full2),
            pl.BlockSpec(bd.shape, full2),
        ],
        out_specs=pl.BlockSpec((1, ts, d), lambda b, i: (b, i, 0)),
        out_shape=jax.ShapeDtypeStruct((bsz, s, d), f32),
        scratch_shapes=[
            pltpu.VMEM((CONF_HIST + ts, D_GRP), f32),
            pltpu.VMEM((SMALL_HIST + ts, D_GRP), f32),
            pltpu.VMEM((SMALL_HIST + ts, D_GRP), f32),
            pltpu.VMEM((SUBLANES, C_COLS), f32),
            pltpu.VMEM((SUBLANES, D_GRP), f32),
            pltpu.VMEM((N_HEADS, HEAD_DIM, HEAD_DIM), f32),
            tile(D_MODEL),
        ],
        compiler_params=pltpu.CompilerParams(
            dimension_semantics=("arbitrary", "arbitrary"),
            vmem_limit_bytes=VMEM_LIMIT_BYTES),
        name="token_mixers",
    )(x, sh, sc, g, w_in, w_out, cw, vec, mu, ln, wgate, wlora, bd)


def _first_index(mask, idx, sentinel):
    return jnp.min(jnp.where(mask, idx, sentinel), axis=0, keepdims=True)


def _router_kernel(x_ref, sh_ref, sc_ref, wr_ref, eb_ref, su_ref, sl_ref,
                   slot_ref, wt_ref, plan_ref):
    x = x_ref[0]
    u = x * (1.0 + sc_ref[0]) + sh_ref[0]
    tm = u.shape[0]
    logits = _dot_f32(wr_ref[...], u, ((1,), (1,)))
    scores = jax.nn.sigmoid(logits)
    biased = scores + eb_ref[...]
    neg_inf = jnp.float32(-jnp.inf)
    sub = lax.broadcasted_iota(jnp.int32, (E_PER_GROUP, tm), 0).astype(jnp.float32)

    groups = [biased[g * E_PER_GROUP:(g + 1) * E_PER_GROUP, :] for g in range(N_GROUPS)]
    gscore = []
    for blk in groups:
        m1 = jnp.max(blk, axis=0, keepdims=True)
        first = _first_index(blk == m1, sub, float(E_PER_GROUP))
        m2 = jnp.max(jnp.where(sub == first, neg_inf, blk), axis=0, keepdims=True)
        gscore.append(m1 + m2)
    gs = jnp.concatenate(gscore, axis=0)
    gidx = lax.broadcasted_iota(jnp.int32, (N_GROUPS, tm), 0).astype(jnp.float32)
    keep = jnp.zeros((N_GROUPS, tm), jnp.float32)
    for _ in range(TOPK_GROUPS):
        m = jnp.max(gs, axis=0, keepdims=True)
        first = _first_index(gs == m, gidx, float(N_GROUPS))
        sel = gidx == first
        keep = jnp.where(sel, 1.0, keep)
        gs = jnp.where(sel, neg_inf, gs)

    masked = [jnp.where(keep[g:g + 1, :] > 0.5, groups[g], neg_inf) for g in range(N_GROUPS)]
    eidx = [sub + float(g * E_PER_GROUP) for g in range(N_GROUPS)]
    chosen = [jnp.zeros((E_PER_GROUP, tm), jnp.float32) for _ in range(N_GROUPS)]
    for _ in range(TOP_K):
        m = functools.reduce(jnp.maximum, [jnp.max(b, axis=0, keepdims=True) for b in masked])
        first = functools.reduce(
            jnp.minimum,
            [_first_index(masked[g] == m, eidx[g], float(N_EXPERTS)) for g in range(N_GROUPS)])
        for g in range(N_GROUPS):
            sel = eidx[g] == first
            chosen[g] = jnp.where(sel, 1.0, chosen[g])
            masked[g] = jnp.where(sel, neg_inf, masked[g])

    picked = [jnp.where(chosen[g] > 0.5, scores[g * E_PER_GROUP:(g + 1) * E_PER_GROUP, :], 0.0)
              for g in range(N_GROUPS)]
    total = functools.reduce(jnp.add, [jnp.sum(p, axis=0, keepdims=True) for p in picked])
    wts = jnp.concatenate([p / total * ROUTED_SCALE for p in picked], axis=0)

    sel = jnp.concatenate(chosen, axis=0)
    sel_bf = _bf16(sel)
    rank = jnp.dot(sel_bf, su_ref[...], preferred_element_type=jnp.float32)
    cnt = jnp.sum(sel, axis=1, keepdims=True)
    off = _dot_f32(sl_ref[...].astype(jnp.float32), jnp.broadcast_to(cnt, (N_EXPERTS, LANES)))
    slot = off[:, 0:1] + rank
    kidx = jnp.dot(sl_ref[...], sel_bf, preferred_element_type=jnp.float32)
    slot_rows, wt_rows = [], []
    for kk in range(TOP_K):
        pick = (sel > 0.5) & (kidx == float(kk))
        slot_rows.append(jnp.sum(jnp.where(pick, slot, 0.0), axis=0, keepdims=True))
        wt_rows.append(jnp.sum(jnp.where(pick, wts, 0.0), axis=0, keepdims=True))
    lane_pad = [jnp.zeros((LANES - TOP_K, tm), jnp.float32)]
    slot_ref[0] = (jnp.concatenate(slot_rows + lane_pad, axis=0).T
                   * float(SLAB_ROWS)).astype(jnp.int32)
    wt_ref[0] = jnp.concatenate(wt_rows + lane_pad, axis=0).T
    e_sub = lax.broadcasted_iota(jnp.int32, (N_EXPERTS, LANES), 0)
    e_lane = lax.broadcasted_iota(jnp.int32, (N_EXPERTS, LANES), 1)
    diag = e_sub == e_lane
    off_row = jnp.sum(jnp.where(diag, off, 0.0), axis=0, keepdims=True)
    cnt_row = jnp.sum(jnp.where(diag, jnp.broadcast_to(cnt, (N_EXPERTS, LANES)), 0.0),
                      axis=0, keepdims=True)
    plan_ref[0] = jnp.concatenate([off_row, cnt_row], axis=0).astype(jnp.int32)


def _router_layer(x, sh, sc, w_router_t, e_bias_col, su, sl, tm):
    bsz, s, d = x.shape
    per_b = s // tm
    n_tiles = bsz * per_b
    mod_spec = pl.BlockSpec((1, 1, d), lambda b, i: (b, 0, 0))
    const2 = lambda b, i: (0, 0)
    tile3 = lambda b, i: (b * per_b + i, 0, 0)
    return pl.pallas_call(
        _router_kernel,
        grid=(bsz, per_b),
        in_specs=[
            pl.BlockSpec((1, tm, d), lambda b, i: (b, i, 0)),
            mod_spec, mod_spec,
            pl.BlockSpec(w_router_t.shape, const2),
            pl.BlockSpec(e_bias_col.shape, const2),
            pl.BlockSpec(su.shape, const2),
            pl.BlockSpec(sl.shape, const2),
        ],
        out_specs=[
            pl.BlockSpec((1, tm, LANES), tile3),
            pl.BlockSpec((1, tm, LANES), tile3),
            pl.BlockSpec((1, 2, LANES), tile3),
        ],
        out_shape=[
            jax.ShapeDtypeStruct((n_tiles, tm, LANES), jnp.int32),
            jax.ShapeDtypeStruct((n_tiles, tm, LANES), jnp.float32),
            jax.ShapeDtypeStruct((n_tiles, 2, LANES), jnp.int32),
        ],
        compiler_params=pltpu.CompilerParams(
            dimension_semantics=("arbitrary", "arbitrary"),
            vmem_limit_bytes=VMEM_LIMIT_BYTES),
        name="moe_router",
    )(x, sh, sc, w_router_t, e_bias_col, su, sl)


def _swiglu(u_bf16, w13, w2):
    hcat = jnp.dot(u_bf16, w13, preferred_element_type=jnp.float32)
    half = hcat.shape[1] // 2
    gate, up = hcat[:, :half], hcat[:, half:]
    act = gate * jax.nn.sigmoid(gate) * up
    return jnp.dot(_bf16(act), w2, preferred_element_type=jnp.float32)


N_DMA_QUEUES = 2
W_COPIES = 2 * N_DMA_QUEUES


def _weight_copies(layer, w13_hbm, w2_hbm, w13_buf, w2_buf, sems, e, slot):
    copies = []
    for hbm, buf in ((w13_hbm, w13_buf), (w2_hbm, w2_buf)):
        rows = hbm.shape[2] // N_DMA_QUEUES
        for part in range(N_DMA_QUEUES):
            sl = pl.ds(part * rows, rows)
            copies.append(pltpu.make_async_copy(hbm.at[layer, e, sl], buf.at[slot, sl],
                                                sems.at[len(copies), slot]))
    return copies


def _start_weight_copies(layer, w13_hbm, w2_hbm, w13_buf, w2_buf, sems, e, slot):
    for n, cp in enumerate(_weight_copies(layer, w13_hbm, w2_hbm, w13_buf, w2_buf, sems, e, slot)):
        cp.start(priority=n % N_DMA_QUEUES)


def _expert_kernel(layer, plan_ref, slot_ref, wt_ref, x_ref, sh_ref, sc_ref, g_ref, w13_hbm,
                   w2_hbm, ws13_ref, ws2_ref, ln_ref, o_ref, rows_f, xf, w13_buf, w2_buf, sems):
    i = pl.program_id(0)
    tm = x_ref.shape[0]
    n_exp = w13_hbm.shape[1]
    n_col = x_ref.shape[1] // LANES

    @pl.when(i == 0)
    def _clear():
        xf[...] = jnp.zeros(xf.shape, xf.dtype)

    for e0 in range(W_AHEAD):
        _start_weight_copies(layer, w13_hbm, w2_hbm, w13_buf, w2_buf, sems, e0, e0)

    u = x_ref[...] * (1.0 + sc_ref[0]) + sh_ref[0]
    for j in range(n_col):
        rows_f[pl.ds(j, tm, stride=SLAB_ROWS), :] = u[:, j * LANES:(j + 1) * LANES]

    def scatter(t, carry):
        row = rows_f[pl.ds(pl.multiple_of(t * SLAB_ROWS, SLAB_ROWS), SLAB_ROWS), :]
        for k in range(TOP_K):
            dst = pl.multiple_of(slot_ref[t * TOP_K + k], SLAB_ROWS)
            xf[pl.ds(dst, SLAB_ROWS), :] = row
        return carry

    lax.fori_loop(0, tm, scatter, 0)

    def expert(e, carry):
        slot = lax.rem(e, W_SLOTS)
        for cp in _weight_copies(layer, w13_hbm, w2_hbm, w13_buf, w2_buf, sems, e, slot):
            cp.wait()

        @pl.when(e + W_AHEAD < n_exp)
        def _prefetch():
            nxt = e + W_AHEAD
            _start_weight_copies(layer, w13_hbm, w2_hbm, w13_buf, w2_buf, sems, nxt,
                                 lax.rem(nxt, W_SLOTS))

        first = plan_ref[i * 2 * LANES + e]
        count = plan_ref[i * 2 * LANES + LANES + e]
        n_blocks = lax.div(count + (EXPERT_ROWS - 1), EXPERT_ROWS)

        def block(j, c2):
            base = (first + j * EXPERT_ROWS) * SLAB_ROWS
            old = [xf[pl.ds(base + q, EXPERT_ROWS, stride=SLAB_ROWS), :] for q in range(n_col)]
            xb = jnp.concatenate([_bf16(o) for o in old], axis=1)
            y = _swiglu(xb, w13_buf[slot], w2_buf[slot])
            live = (lax.broadcasted_iota(jnp.int32, (EXPERT_ROWS, LANES), 0)
                    < count - j * EXPERT_ROWS)
            for q in range(n_col):
                xf[pl.ds(base + q, EXPERT_ROWS, stride=SLAB_ROWS), :] = jnp.where(
                    live, y[:, q * LANES:(q + 1) * LANES], old[q])
            return c2

        lax.fori_loop(0, n_blocks, block, 0)
        return carry

    lax.fori_loop(0, n_exp, expert, 0)

    def gather(t, carry):
        acc = jnp.zeros((SLAB_ROWS, LANES), jnp.float32)
        for k in range(TOP_K):
            src = pl.multiple_of(slot_ref[t * TOP_K + k], SLAB_ROWS)
            acc = acc + wt_ref[t * TOP_K + k] * xf[pl.ds(src, SLAB_ROWS), :]
        rows_f[pl.ds(pl.multiple_of(t * SLAB_ROWS, SLAB_ROWS), SLAB_ROWS), :] = acc
        return carry

    lax.fori_loop(0, tm, gather, 0)
    routed = jnp.concatenate([rows_f[pl.ds(j, tm, stride=SLAB_ROWS), :] for j in range(n_col)],
                             axis=1)
    x = x_ref[...]
    u = x * (1.0 + sc_ref[0]) + sh_ref[0]
    ffn = routed + _swiglu(_bf16(u), ws13_ref[...], ws2_ref[...])
    res = ln_ref[2:3, :] * x + (1.0 + g_ref[0]) * ffn
    o_ref[...] = _layer_norm(res, ln_ref[0:1, :], ln_ref[1:2, :])


def _expert_layer(x, sh, sc, g, plan, slots, wts, layer, w13, w2, ws13, ws2, ln, tm):
    bsz, s, d = x.shape
    per_b = s // tm
    n_tiles = bsz * per_b
    mod_spec = pl.BlockSpec((1, 1, d), lambda i, plan: (i // per_b, 0, 0))
    full2 = lambda i, plan: (0, 0)
    once = pl.Buffered(1)
    tile_smem = pl.BlockSpec((TOP_K * tm,), lambda i, plan: (i,), memory_space=pltpu.SMEM)
    slab_rows = (TOP_K * tm + EXPERT_ROWS) * SLAB_ROWS
    grid_spec = pltpu.PrefetchScalarGridSpec(
        num_scalar_prefetch=1,
        grid=(n_tiles,),
        in_specs=[
            tile_smem, tile_smem,
            pl.BlockSpec((tm, d), lambda i, plan: (i, 0), pipeline_mode=once),
            mod_spec, mod_spec, mod_spec,
            pl.BlockSpec(memory_space=pl.ANY),
            pl.BlockSpec(memory_space=pl.ANY),
            pl.BlockSpec(ws13.shape, full2, pipeline_mode=once),
            pl.BlockSpec(ws2.shape, full2, pipeline_mode=once),
            pl.BlockSpec(ln.shape, full2),
        ],
        out_specs=pl.BlockSpec((tm, d), lambda i, plan: (i, 0), pipeline_mode=once),
        scratch_shapes=[
            pltpu.VMEM((tm * SLAB_ROWS, LANES), jnp.float32),
            pltpu.VMEM((slab_rows, LANES), jnp.float32),
            pltpu.VMEM((W_SLOTS,) + w13.shape[2:], w13.dtype),
            pltpu.VMEM((W_SLOTS,) + w2.shape[2:], w2.dtype),
            pltpu.SemaphoreType.DMA((W_COPIES, W_SLOTS)),
        ],
    )
    out = pl.pallas_call(
        functools.partial(_expert_kernel, layer),
        grid_spec=grid_spec,
        out_shape=jax.ShapeDtypeStruct((bsz * s, d), jnp.float32),
        compiler_params=pltpu.CompilerParams(
            dimension_semantics=("arbitrary",),
            vmem_limit_bytes=VMEM_LIMIT_BYTES),
        name="moe_experts",
    )(plan.reshape(-1), slots[:, :, :TOP_K].reshape(-1), wts[:, :, :TOP_K].reshape(-1),
      x.reshape(bsz * s, d), sh, sc, g,
      w13, w2, ws13, ws2, ln)
    return out.reshape(bsz, s, d)


def _block_diag(w):
    h, n, _ = w.shape
    eye = jnp.eye(h, dtype=w.dtype)
    return (eye[:, None, :, None] * w[:, :, None, :]).reshape(h * n, h * n)


def _pad_rows(a, n_rows):
    return jnp.concatenate([a, jnp.zeros((n_rows - a.shape[0], a.shape[1]), a.dtype)], axis=0)


def _tile_rows(seq_len, want):
    t = min(want, seq_len)
    assert seq_len % t == 0 and t % RWKV_CHUNK == 0
    return t


def kernel(x, c, w_mod, b_mod, w_in, w_out, conv_a, conv_a_bias, ln_a_g, ln_a_b, conv_b, conv_b_bias, w_rg, b_rg, w_ig, b_ig, lru_lambda, mu_c, w0, w_w2, a0, w_a2, w_g2, k_k, k_a, r_k, gn_g, gn_b, conv_d, ln1_g, ln1_b, w_router, e_bias, w13, w2, ws13, ws2, ln2_g, ln2_b):
    n_layers = w_mod.shape[0]
    bsz, s, d = x.shape
    alpha_dn = (2.0 * n_layers) ** 0.25
    ts_mix = _tile_rows(s, 512)
    tm_moe = _tile_rows(s, 1024)
    bf = jnp.bfloat16

    mod = _modulation(c, w_mod, b_mod)
    bd = _block_diag(jnp.ones((N_HEADS, HEAD_DIM, HEAD_DIM), bf))
    alpha_row = jnp.full((1, d), alpha_dn, jnp.float32)
    tri_tok = jnp.triu(jnp.ones((tm_moe, tm_moe), bf), k=1)
    tri_exp = jnp.tril(jnp.ones((N_EXPERTS, N_EXPERTS), bf), k=-1)
    w13_bf, w2_bf = w13.astype(bf), w2.astype(bf)

    for l in range(n_layers):
        sh1, sc1, g1, sh2, sc2, g2 = [mod[l, :, i * d:(i + 1) * d].reshape(bsz, 1, d)
                                      for i in range(6)]
        cw = _pad_rows(jnp.concatenate([conv_a[l], conv_b[l], conv_d[l]], axis=0), N_CW_ROWS)
        vec = _pad_rows(jnp.stack([
            conv_a_bias[l], ln_a_g[l], ln_a_b[l], conv_b_bias[l], b_rg[l], b_ig[l], lru_lambda[l],
            w0[l], a0[l], k_k[l], k_a[l], r_k[l].reshape(D_GRP), gn_g[l], gn_b[l]], axis=0),
            N_VEC_ROWS)
        wgate = jnp.concatenate([_block_diag(w_rg[l]), _block_diag(w_ig[l])], axis=1).astype(bf)
        wlora = jnp.zeros((LORA_W + LORA_A + LORA_G, 3 * D_GRP), jnp.float32)
        wlora = wlora.at[0:LORA_W, 0:D_GRP].set(w_w2[l])
        wlora = wlora.at[LORA_W:LORA_W + LORA_A, D_GRP:2 * D_GRP].set(w_a2[l])
        wlora = wlora.at[LORA_W + LORA_A:, 2 * D_GRP:].set(w_g2[l]).astype(bf)
        ln1 = jnp.concatenate([ln1_g[l][None], ln1_b[l][None], alpha_row], axis=0)
        ln2 = jnp.concatenate([ln2_g[l][None], ln2_b[l][None], alpha_row], axis=0)

        x = _token_mixer_layer(x, sh1, sc1, g1, w_in[l].astype(bf), w_out[l].astype(bf), cw, vec,
                               mu_c[l][None], ln1, wgate, wlora, bd, ts_mix)
        slots, wts, plan = _router_layer(x, sh2, sc2, w_router[l].T, e_bias[l][:, None],
                                         tri_tok, tri_exp, tm_moe)
        x = _expert_layer(x, sh2, sc2, g2, plan, slots, wts, l, w13_bf, w2_bf,
                          ws13[l].astype(bf), ws2[l].astype(bf), ln2, tm_moe)
    return x
```

```python
import functools

import jax
import jax.numpy as jnp
from jax import lax
from jax.experimental import pallas as pl
from jax.experimental.pallas import tpu as pltpu

D_MODEL = 1024
HEAD_DIM = 64
D_GRP = 256
N_HEADS = D_GRP // HEAD_DIM
CONF_KERNEL = 31
LRU_CONV = 4
LRU_C = 8.0
SHORT_CONV = 3
LORA_W, LORA_A, LORA_G = 32, 32, 64
C_COLS = 3 * D_GRP + LORA_W + LORA_A + LORA_G
P_IN = 4 * D_GRP + C_COLS + 3 * D_GRP
RWKV_GN_EPS = 64e-5
N_EXPERTS = 64
TOP_K = 8
N_GROUPS = 8
TOPK_GROUPS = 4
E_PER_GROUP = N_EXPERTS // N_GROUPS
D_EXPERT = 256
ROUTED_SCALE = 2.5
LN_EPS = 1e-5

OFF_A = 0
OFF_B = 2 * D_GRP
OFF_C = 4 * D_GRP
OFF_D = OFF_C + C_COLS

SUBLANES = 8
LANES = 128
VMEM_LIMIT_BYTES = 58 * 1024 * 1024

SLAB_ROWS = D_MODEL // LANES
EXPERT_ROWS = 144
ROW_BLOCK = 256
W_SLOTS = 6
W_AHEAD = W_SLOTS - 1
RWKV_CHUNK = 64
CONF_HIST = 32
SMALL_HIST = 8

(V_CONV_A_BIAS, V_LN_A_G, V_LN_A_B, V_CONV_B_BIAS, V_B_RG, V_B_IG, V_LRU_LAMBDA, V_W0, V_A0,
 V_K_K, V_K_A, V_R_K, V_GN_G, V_GN_B) = range(14)
N_VEC_ROWS = 16
CW_A, CW_B, CW_D = 0, CONF_KERNEL, CONF_KERNEL + LRU_CONV
N_CW_ROWS = 40


def _bf16(x):
    return x.astype(jnp.bfloat16)


def _dot(a, b, dims=((1,), (0,))):
    return lax.dot_general(_bf16(a), _bf16(b), (dims, ((), ())),
                           preferred_element_type=jnp.float32)


def _split(a):
    hi = _bf16(a)
    lo = _bf16(a - hi.astype(jnp.float32))
    return hi, lo


def _dot_f32(a, b, dims=((1,), (0,))):
    a_hi, a_lo = _split(a)
    b_hi, b_lo = _split(b)
    dn = (dims, ((), ()))
    f = functools.partial(lax.dot_general, dimension_numbers=dn,
                          preferred_element_type=jnp.float32)
    return f(a_hi, b_hi) + (f(a_hi, b_lo) + f(a_lo, b_hi))


def _dot_lhs_f32(a, b_bf16):
    a_hi, a_lo = _split(a)
    f = functools.partial(jnp.dot, preferred_element_type=jnp.float32)
    return f(a_hi, b_bf16) + f(a_lo, b_bf16)


def _layer_norm(x, g, b):
    mu = jnp.mean(x, axis=-1, keepdims=True)
    xc = x - mu
    var = jnp.mean(xc * xc, axis=-1, keepdims=True)
    return xc * lax.rsqrt(var + LN_EPS) * g + b


def _softplus(x):
    return jnp.maximum(x, 0.0) + jnp.log1p(jnp.exp(-jnp.abs(x)))


def _row_iota(shape):
    return lax.broadcasted_iota(jnp.int32, shape, 0)


def _shift_rows(x, d, fill):
    rolled = pltpu.roll(x, d, axis=0)
    return jnp.where(_row_iota(x.shape) >= d, rolled, fill)


def _mod_kernel(c_ref, w_ref, b_ref, o_ref):
    c = c_ref[...]
    c_act = c * jax.nn.sigmoid(c)
    o_ref[0] = _dot_f32(c_act, w_ref[0]) + b_ref[0]


def _modulation(c, w_mod, b_mod):
    n_layers, d, d6 = w_mod.shape
    bsz = c.shape[0]
    tn = D_MODEL
    return pl.pallas_call(
        _mod_kernel,
        grid=(n_layers, d6 // tn),
        in_specs=[
            pl.BlockSpec((bsz, d), lambda l, j: (0, 0)),
            pl.BlockSpec((1, d, tn), lambda l, j: (l, 0, j)),
            pl.BlockSpec((1, 1, tn), lambda l, j: (l, 0, j)),
        ],
        out_specs=pl.BlockSpec((1, bsz, tn), lambda l, j: (l, 0, j)),
        out_shape=jax.ShapeDtypeStruct((n_layers, bsz, d6), jnp.float32),
        compiler_params=pltpu.CompilerParams(
            dimension_semantics=("arbitrary", "arbitrary"),
            vmem_limit_bytes=VMEM_LIMIT_BYTES),
        name="adaln_modulation",
    )(c, w_mod, b_mod.reshape(n_layers, 1, d6))


def _causal_conv(buf_ref, hist, ts, w_rows, n_taps):
    acc = None
    for k in range(n_taps):
        off = hist - (n_taps - 1) + k
        term = w_rows[k:k + 1, :] * buf_ref[pl.ds(off, ts), :]
        acc = term if acc is None else acc + term
    return acc


def _causal_conv_long(buf_ref, hist, ts, w_rows, n_taps):
    ext = ts + SUBLANES
    out = None
    for r in range(SUBLANES):
        part = None
        for q in range((n_taps - 1 - r) // SUBLANES + 1):
            lag = SUBLANES * q + r
            term = (w_rows[n_taps - 1 - lag:n_taps - lag, :]
                    * buf_ref[pl.ds(hist - SUBLANES - SUBLANES * q, ext), :])
            part = term if part is None else part + term
        shifted = part if r == 0 else pltpu.roll(part, r, axis=0)
        piece = shifted[SUBLANES:, :]
        out = piece if out is None else out + piece
    return out


def _head_sum(x, bd_ref):
    return _dot_lhs_f32(x, bd_ref[...])


_NN = ((2,), (1,))
_NT = ((2,), (2,))
_TN = ((1,), (1,))


def _bdot(a, b, dims):
    return lax.dot_general(_bf16(a), _bf16(b), (dims, ((0,), (0,))),
                           preferred_element_type=jnp.float32)


def _to_problems(x):
    n_chunks = x.shape[0] // RWKV_CHUNK
    return jnp.stack([x[c * RWKV_CHUNK:(c + 1) * RWKV_CHUNK, h * HEAD_DIM:(h + 1) * HEAD_DIM]
                      for c in range(n_chunks) for h in range(N_HEADS)], axis=0)


def _rwkv_chunked(kk, r, alpha, kf, v, lw, state):
    ts = kk.shape[0]
    n_chunks = ts // RWKV_CHUNK
    cl = lw
    seg_row = _row_iota(cl.shape) % RWKV_CHUNK
    d = 1
    while d < RWKV_CHUNK:
        cl = cl + jnp.where(seg_row >= d, pltpu.roll(cl, d, axis=0), 0.0)
        d *= 2
    cl_end = cl.reshape(n_chunks, RWKV_CHUNK, D_GRP)[:, RWKV_CHUNK - 1:RWKV_CHUNK, :]
    cl_end_rows = jnp.broadcast_to(cl_end, (n_chunks, RWKV_CHUNK, D_GRP)).reshape(ts, D_GRP)
    g_inv = jnp.exp(-cl)
    tail = jnp.exp(cl_end_rows - cl)
    g_end = jnp.exp(cl_end)
    bt = _to_problems(_bf16(kk * jnp.exp(cl - lw)))
    rt = _to_problems(_bf16(r * jnp.exp(cl)))
    at = _to_problems(_bf16(alpha * g_inv))
    kt = _to_problems(_bf16(kf * g_inv))
    ab = _to_problems(_bf16(alpha * tail))
    kb = _to_problems(_bf16(kf * tail))
    vp = _to_problems(_bf16(v))

    c_sz = RWKV_CHUNK
    tri_r = lax.broadcasted_iota(jnp.int32, (1, c_sz, c_sz), 1)
    tri_c = lax.broadcasted_iota(jnp.int32, (1, c_sz, c_sz), 2)
    strict = tri_r > tri_c
    incl = tri_r >= tri_c
    eye = (tri_r == tri_c).astype(jnp.float32)

    amat = _bdot(jnp.concatenate([bt, rt], axis=1), jnp.concatenate([at, kt], axis=1), _NT)
    a_ba = jnp.where(strict, amat[:, :c_sz, :c_sz], 0.0)
    a_bk = jnp.where(strict, amat[:, :c_sz, c_sz:], 0.0)
    a_ra = jnp.where(incl, amat[:, c_sz:, :c_sz], 0.0)
    a_rk = jnp.where(incl, amat[:, c_sz:, c_sz:], 0.0)
    tinv = eye + a_ba
    pw = a_ba
    step = 2
    while step < c_sz:
        pw = _bdot(pw, pw, _NN)
        tinv = tinv + _bdot(pw, tinv, _NN)
        step *= 2
    tx = _bdot(tinv, jnp.concatenate([bt, _bf16(_bdot(a_bk, vp, _NN))], axis=2), _NN)
    w_t, u_t = tx[:, :, :HEAD_DIM], tx[:, :, HEAD_DIM:]
    y_k = _bdot(a_rk, vp, _NN)

    s_cur = state[...]
    y_rows = []
    for c in range(n_chunks):
        p = slice(c * N_HEADS, (c + 1) * N_HEADS)
        res = _bdot(jnp.concatenate([_bf16(w_t[p]), rt[p]], axis=1), s_cur, _NT)
        u_c = res[:, :c_sz, :] + u_t[p]
        y_c = res[:, c_sz:, :] + _bdot(a_ra[p], u_c, _NN) + y_k[p]
        g_c = jnp.stack([g_end[c, :, h * HEAD_DIM:(h + 1) * HEAD_DIM] for h in range(N_HEADS)],
                        axis=0)
        s_cur = s_cur * g_c + _bdot(jnp.concatenate([_bf16(u_c), vp[p]], axis=1),
                                    jnp.concatenate([ab[p], kb[p]], axis=1), _TN)
        y_rows.append(jnp.concatenate([y_c[h] for h in range(N_HEADS)], axis=1))
    state[...] = s_cur
    return jnp.concatenate(y_rows, axis=0)


def _mixer_kernel(x_ref, sh_ref, sc_ref, g_ref, w_in_ref, w_out_ref, cw_ref, vec_ref, mu_ref,
                  ln_ref, wgate_ref, wlora_ref, bd_ref, o_ref,
                  hbuf, xbbuf, zbuf, pc_last, lru_h, state, ymix):
    ts = x_ref.shape[1]
    s_idx = pl.program_id(1)

    @pl.when(s_idx == 0)
    def _reset():
        hbuf[pl.ds(0, CONF_HIST), :] = jnp.zeros((CONF_HIST, D_GRP), jnp.float32)
        xbbuf[pl.ds(0, SMALL_HIST), :] = jnp.zeros((SMALL_HIST, D_GRP), jnp.float32)
        zbuf[pl.ds(0, SMALL_HIST), :] = jnp.zeros((SMALL_HIST, D_GRP), jnp.float32)
        pc_last[...] = jnp.zeros(pc_last.shape, jnp.float32)
        lru_h[...] = jnp.zeros(lru_h.shape, jnp.float32)
        state[...] = jnp.zeros(state.shape, jnp.float32)

    x = x_ref[0]
    u = x * (1.0 + sc_ref[0]) + sh_ref[0]
    proj = jnp.dot(_bf16(u), w_in_ref[...], preferred_element_type=jnp.float32)

    vec = vec_ref[...]

    def vrow(i):
        return vec[i:i + 1, :]

    cw = cw_ref[...]

    val = proj[:, OFF_A:OFF_A + D_GRP]
    gate = proj[:, OFF_A + D_GRP:OFF_A + 2 * D_GRP]
    hbuf[pl.ds(CONF_HIST, ts), :] = val * jax.nn.sigmoid(gate)
    conv = _causal_conv_long(hbuf, CONF_HIST, ts, cw[CW_A:CW_A + CONF_KERNEL], CONF_KERNEL)
    conv = conv + vrow(V_CONV_A_BIAS)
    hbuf[pl.ds(0, CONF_HIST), :] = hbuf[pl.ds(ts, CONF_HIST), :]
    ln_a = _layer_norm(conv, vrow(V_LN_A_G), vrow(V_LN_A_B))
    ymix[:, 0:D_GRP] = ln_a * jax.nn.sigmoid(ln_a)

    xbbuf[pl.ds(SMALL_HIST, ts), :] = proj[:, OFF_B:OFF_B + D_GRP]
    gb = proj[:, OFF_B + D_GRP:OFF_B + 2 * D_GRP]
    ub = _causal_conv(xbbuf, SMALL_HIST, ts, cw[CW_B:CW_B + LRU_CONV], LRU_CONV)
    ub = ub + vrow(V_CONV_B_BIAS)
    xbbuf[pl.ds(0, SMALL_HIST), :] = xbbuf[pl.ds(ts, SMALL_HIST), :]
    gates = jnp.dot(_bf16(ub), wgate_ref[...], preferred_element_type=jnp.float32)
    r_gate = jax.nn.sigmoid(gates[:, :D_GRP] + vrow(V_B_RG))
    i_gate = jax.nn.sigmoid(gates[:, D_GRP:] + vrow(V_B_IG))
    log_a = (-LRU_C) * r_gate * _softplus(-vrow(V_LRU_LAMBDA))
    a_sc = jnp.exp(log_a)
    b_sc = jnp.sqrt(-jnp.tanh(log_a) * (a_sc * a_sc + 1.0)) * (i_gate * ub)
    d = 1
    while d < ts:
        a_sh = _shift_rows(a_sc, d, 1.0)
        b_sh = _shift_rows(b_sc, d, 0.0)
        b_sc = a_sc * b_sh + b_sc
        a_sc = a_sc * a_sh
        d *= 2
    h = b_sc + a_sc * lru_h[0:1, :]
    lru_h[0:1, :] = h[ts - 1:ts, :]
    ymix[:, D_GRP:2 * D_GRP] = h * jax.nn.gelu(gb, approximate=True)

    gbd = proj[:, OFF_D:OFF_D + D_GRP]
    zbuf[pl.ds(SMALL_HIST, ts), :] = (proj[:, OFF_D + D_GRP:OFF_D + 2 * D_GRP]
                                      * proj[:, OFF_D + 2 * D_GRP:OFF_D + 3 * D_GRP])
    convd = _causal_conv(zbuf, SMALL_HIST, ts, cw[CW_D:CW_D + SHORT_CONV], SHORT_CONV)
    zbuf[pl.ds(0, SMALL_HIST), :] = zbuf[pl.ds(ts, SMALL_HIST), :]
    ymix[:, 3 * D_GRP:4 * D_GRP] = gbd * convd

    pc = proj[:, OFF_C:OFF_C + C_COLS]
    prev = jnp.where(_row_iota(pc.shape) == 0, pc_last[0:1, :], pltpu.roll(pc, 1, axis=0))
    pc_last[0:1, :] = pc[ts - 1:ts, :]
    xs = pc + (prev - pc) * mu_ref[...]
    r = xs[:, 0:D_GRP]
    k = xs[:, D_GRP:2 * D_GRP]
    v = xs[:, 2 * D_GRP:3 * D_GRP]
    z = xs[:, 3 * D_GRP:C_COLS]
    lane = lax.broadcasted_iota(jnp.int32, z.shape, 1)
    z_act = jnp.where(lane < LORA_W, jnp.tanh(z),
                      jnp.where(lane < LORA_W + LORA_A, z, jax.nn.sigmoid(z)))
    lora = jnp.dot(_bf16(z_act), wlora_ref[...], preferred_element_type=jnp.float32)
    wlog = -_softplus(-(vrow(V_W0) + lora[:, 0:D_GRP])) - 0.5
    lw = -jnp.exp(wlog)
    a_icl = jax.nn.sigmoid(vrow(V_A0) + lora[:, D_GRP:2 * D_GRP])
    g_out = lora[:, 2 * D_GRP:3 * D_GRP]

    kk = k * vrow(V_K_K)
    kk = kk * lax.rsqrt(_head_sum(kk * kk, bd_ref) + 1e-12)
    kf = k * (1.0 + (a_icl - 1.0) * vrow(V_K_A))

    y = _rwkv_chunked(kk, r, -(kk * a_icl), kf, v, lw, state)
    inv_n = 1.0 / HEAD_DIM
    mu_y = _head_sum(y, bd_ref) * inv_n
    yc = y - mu_y
    var_y = _head_sum(yc * yc, bd_ref) * inv_n
    y = yc * lax.rsqrt(var_y + RWKV_GN_EPS) * vrow(V_GN_G) + vrow(V_GN_B)
    bonus = _head_sum(r * kf * vrow(V_R_K), bd_ref) * v
    ymix[:, 2 * D_GRP:3 * D_GRP] = (y + bonus) * g_out

    mix = jnp.dot(_bf16(ymix[...]), w_out_ref[...], preferred_element_type=jnp.float32)
    alpha_dn = ln_ref[2:3, :]
    res = alpha_dn * x + (1.0 + g_ref[0]) * mix
    o_ref[0] = _layer_norm(res, ln_ref[0:1, :], ln_ref[1:2, :])


def _token_mixer_layer(x, sh, sc, g, w_in, w_out, cw, vec, mu, ln, wgate, wlora, bd, ts):
    bsz, s, d = x.shape
    grid = (bsz, s // ts)
    full2 = lambda b, i: (0, 0)
    mod_spec = pl.BlockSpec((1, 1, d), lambda b, i: (b, 0, 0))
    f32 = jnp.float32

    def tile(n_cols):
        return pltpu.VMEM((ts, n_cols), f32)

    return pl.pallas_call(
        _mixer_kernel,
        grid=grid,
        in_specs=[
            pl.BlockSpec((1, ts, d), lambda b, i: (b, i, 0)),
            mod_spec, mod_spec, mod_spec,
            pl.BlockSpec(w_in.shape, full2),
            pl.BlockSpec(w_out.shape, full2),
            pl.BlockSpec(cw.shape, full2),
            pl.BlockSpec(vec.shape, full2),
            pl.BlockSpec(mu.shape, full2),
            pl.BlockSpec(ln.shape, full2),
            pl.BlockSpec(wgate.shape, full2),
            pl.BlockSpec(wlora.shape, full2),
            pl.BlockSpec(bd.shape, full2),
        ],
        out_specs=pl.BlockSpec((1, ts, d), lambda b, i: (b, i, 0)),
        out_shape=jax.ShapeDtypeStruct((bsz, s, d), f32),
        scratch_shapes=[
            pltpu.VMEM((CONF_HIST + ts, D_GRP), f32),
            pltpu.VMEM((SMALL_HIST + ts, D_GRP), f32),
            pltpu.VMEM((SMALL_HIST + ts, D_GRP), f32),
            pltpu.VMEM((SUBLANES, C_COLS), f32),
            pltpu.VMEM((SUBLANES, D_GRP), f32),
            pltpu.VMEM((N_HEADS, HEAD_DIM, HEAD_DIM), f32),
            tile(D_MODEL),
        ],
        compiler_params=pltpu.CompilerParams(
            dimension_semantics=("arbitrary", "arbitrary"),
            vmem_limit_bytes=VMEM_LIMIT_BYTES),
        name="token_mixers",
    )(x, sh, sc, g, w_in, w_out, cw, vec, mu, ln, wgate, wlora, bd)


def _first_index(mask, idx, sentinel):
    return jnp.min(jnp.where(mask, idx, sentinel), axis=0, keepdims=True)


def _router_kernel(x_ref, sh_ref, sc_ref, wr_ref, eb_ref, su_ref, sl_ref,
                   slot_ref, wt_ref, plan_ref):
    x = x_ref[0]
    u = x * (1.0 + sc_ref[0]) + sh_ref[0]
    tm = u.shape[0]
    logits = _dot_f32(wr_ref[...], u, ((1,), (1,)))
    scores = jax.nn.sigmoid(logits)
    biased = scores + eb_ref[...]
    neg_inf = jnp.float32(-jnp.inf)
    sub = lax.broadcasted_iota(jnp.int32, (E_PER_GROUP, tm), 0).astype(jnp.float32)

    groups = [biased[g * E_PER_GROUP:(g + 1) * E_PER_GROUP, :] for g in range(N_GROUPS)]
    gscore = []
    for blk in groups:
        m1 = jnp.max(blk, axis=0, keepdims=True)
        first = _first_index(blk == m1, sub, float(E_PER_GROUP))
        m2 = jnp.max(jnp.where(sub == first, neg_inf, blk), axis=0, keepdims=True)
        gscore.append(m1 + m2)
    gs = jnp.concatenate(gscore, axis=0)
    gidx = lax.broadcasted_iota(jnp.int32, (N_GROUPS, tm), 0).astype(jnp.float32)
    keep = jnp.zeros((N_GROUPS, tm), jnp.float32)
    for _ in range(TOPK_GROUPS):
        m = jnp.max(gs, axis=0, keepdims=True)
        first = _first_index(gs == m, gidx, float(N_GROUPS))
        sel = gidx == first
        keep = jnp.where(sel, 1.0, keep)
        gs = jnp.where(sel, neg_inf, gs)

    masked = [jnp.where(keep[g:g + 1, :] > 0.5, groups[g], neg_inf) for g in range(N_GROUPS)]
    eidx = [sub + float(g * E_PER_GROUP) for g in range(N_GROUPS)]
    chosen = [jnp.zeros((E_PER_GROUP, tm), jnp.float32) for _ in range(N_GROUPS)]
    for _ in range(TOP_K):
        m = functools.reduce(jnp.maximum, [jnp.max(b, axis=0, keepdims=True) for b in masked])
        first = functools.reduce(
            jnp.minimum,
            [_first_index(masked[g] == m, eidx[g], float(N_EXPERTS)) for g in range(N_GROUPS)])
        for g in range(N_GROUPS):
            sel = eidx[g] == first
            chosen[g] = jnp.where(sel, 1.0, chosen[g])
            masked[g] = jnp.where(sel, neg_inf, masked[g])

    picked = [jnp.where(chosen[g] > 0.5, scores[g * E_PER_GROUP:(g + 1) * E_PER_GROUP, :], 0.0)
              for g in range(N_GROUPS)]
    total = functools.reduce(jnp.add, [jnp.sum(p, axis=0, keepdims=True) for p in picked])
    wts = jnp.concatenate([p / total * ROUTED_SCALE for p in picked], axis=0)

    sel = jnp.concatenate(chosen, axis=0)
    sel_bf = _bf16(sel)
    rank = jnp.dot(sel_bf, su_ref[...], preferred_element_type=jnp.float32)
    cnt = jnp.sum(sel, axis=1, keepdims=True)
    off = _dot_f32(sl_ref[...].astype(jnp.float32), jnp.broadcast_to(cnt, (N_EXPERTS, LANES)))
    slot = off[:, 0:1] + rank
    kidx = jnp.dot(sl_ref[...], sel_bf, preferred_element_type=jnp.float32)
    slot_rows, wt_rows = [], []
    for kk in range(TOP_K):
        pick = (sel > 0.5) & (kidx == float(kk))
        slot_rows.append(jnp.sum(jnp.where(pick, slot, 0.0), axis=0, keepdims=True))
        wt_rows.append(jnp.sum(jnp.where(pick, wts, 0.0), axis=0, keepdims=True))
    lane_pad = [jnp.zeros((LANES - TOP_K, tm), jnp.float32)]
    slot_ref[0] = (jnp.concatenate(slot_rows + lane_pad, axis=0).T
                   * float(SLAB_ROWS)).astype(jnp.int32)
    wt_ref[0] = jnp.concatenate(wt_rows + lane_pad, axis=0).T
    e_sub = lax.broadcasted_iota(jnp.int32, (N_EXPERTS, LANES), 0)
    e_lane = lax.broadcasted_iota(jnp.int32, (N_EXPERTS, LANES), 1)
    diag = e_sub == e_lane
    off_row = jnp.sum(jnp.where(diag, off, 0.0), axis=0, keepdims=True)
    cnt_row = jnp.sum(jnp.where(diag, jnp.broadcast_to(cnt, (N_EXPERTS, LANES)), 0.0),
                      axis=0, keepdims=True)
    plan_ref[0] = jnp.concatenate([off_row, cnt_row], axis=0).astype(jnp.int32)


def _router_layer(x, sh, sc, w_router_t, e_bias_col, su, sl, tm):
    bsz, s, d = x.shape
    per_b = s // tm
    n_tiles = bsz * per_b
    mod_spec = pl.BlockSpec((1, 1, d), lambda b, i: (b, 0, 0))
    const2 = lambda b, i: (0, 0)
    tile3 = lambda b, i: (b * per_b + i, 0, 0)
    return pl.pallas_call(
        _router_kernel,
        grid=(bsz, per_b),
        in_specs=[
            pl.BlockSpec((1, tm, d), lambda b, i: (b, i, 0)),
            mod_spec, mod_spec,
            pl.BlockSpec(w_router_t.shape, const2),
            pl.BlockSpec(e_bias_col.shape, const2),
            pl.BlockSpec(su.shape, const2),
            pl.BlockSpec(sl.shape, const2),
        ],
        out_specs=[
            pl.BlockSpec((1, tm, LANES), tile3),
            pl.BlockSpec((1, tm, LANES), tile3),
            pl.BlockSpec((1, 2, LANES), tile3),
        ],
        out_shape=[
            jax.ShapeDtypeStruct((n_tiles, tm, LANES), jnp.int32),
            jax.ShapeDtypeStruct((n_tiles, tm, LANES), jnp.float32),
            jax.ShapeDtypeStruct((n_tiles, 2, LANES), jnp.int32),
        ],
        compiler_params=pltpu.CompilerParams(
            dimension_semantics=("arbitrary", "arbitrary"),
            vmem_limit_bytes=VMEM_LIMIT_BYTES),
        name="moe_router",
    )(x, sh, sc, w_router_t, e_bias_col, su, sl)


def _swiglu(u_bf16, w13, w2):
    hcat = jnp.dot(u_bf16, w13, preferred_element_type=jnp.float32)
    half = hcat.shape[1] // 2
    gate, up = hcat[:, :half], hcat[:, half:]
    act = gate * jax.nn.sigmoid(gate) * up
    return jnp.dot(_bf16(act), w2, preferred_element_type=jnp.float32)


N_DMA_QUEUES = 2
W_COPIES = 2 * N_DMA_QUEUES


def _weight_copies(layer, w13_hbm, w2_hbm, w13_buf, w2_buf, sems, e, slot):
    copies = []
    for hbm, buf in ((w13_hbm, w13_buf), (w2_hbm, w2_buf)):
        rows = hbm.shape[2] // N_DMA_QUEUES
        for part in range(N_DMA_QUEUES):
            sl = pl.ds(part * rows, rows)
            copies.append(pltpu.make_async_copy(hbm.at[layer, e, sl], buf.at[slot, sl],
                                                sems.at[len(copies), slot]))
    return copies


def _start_weight_copies(layer, w13_hbm, w2_hbm, w13_buf, w2_buf, sems, e, slot):
    for n, cp in enumerate(_weight_copies(layer, w13_hbm, w2_hbm, w13_buf, w2_buf, sems, e, slot)):
        cp.start(priority=n % N_DMA_QUEUES)


def _expert_kernel(layer, plan_ref, slot_ref, wt_ref, x_ref, sh_ref, sc_ref, g_ref, w13_hbm,
                   w2_hbm, ws13_ref, ws2_ref, ln_ref, o_ref, rows_f, xf, w13_buf, w2_buf, sems):
    i = pl.program_id(0)
    tm = x_ref.shape[0]
    n_exp = w13_hbm.shape[1]
    n_col = x_ref.shape[1] // LANES

    @pl.when(i == 0)
    def _clear():
        xf[...] = jnp.zeros(xf.shape, xf.dtype)

    for e0 in range(W_AHEAD):
        _start_weight_copies(layer, w13_hbm, w2_hbm, w13_buf, w2_buf, sems, e0, e0)

    def modulate(r0):
        x_blk = x_ref[pl.ds(r0, ROW_BLOCK), :]
        return x_blk, x_blk * (1.0 + sc_ref[0]) + sh_ref[0]

    def to_rows(b, carry):
        r0 = pl.multiple_of(b * ROW_BLOCK, ROW_BLOCK)
        _, u = modulate(r0)
        for j in range(n_col):
            rows_f[pl.ds(r0 * SLAB_ROWS + j, ROW_BLOCK, stride=SLAB_ROWS), :] = (
                u[:, j * LANES:(j + 1) * LANES])
        return carry

    lax.fori_loop(0, tm // ROW_BLOCK, to_rows, 0)

    def scatter(t, carry):
        row = rows_f[pl.ds(pl.multiple_of(t * SLAB_ROWS, SLAB_ROWS), SLAB_ROWS), :]
        for k in range(TOP_K):
            dst = pl.multiple_of(slot_ref[t * TOP_K + k], SLAB_ROWS)
            xf[pl.ds(dst, SLAB_ROWS), :] = row
        return carry

    lax.fori_loop(0, tm, scatter, 0)

    def expert(e, carry):
        slot = lax.rem(e, W_SLOTS)
        for cp in _weight_copies(layer, w13_hbm, w2_hbm, w13_buf, w2_buf, sems, e, slot):
            cp.wait()

        @pl.when(e + W_AHEAD < n_exp)
        def _prefetch():
            nxt = e + W_AHEAD
            _start_weight_copies(layer, w13_hbm, w2_hbm, w13_buf, w2_buf, sems, nxt,
                                 lax.rem(nxt, W_SLOTS))

        first = plan_ref[i * 2 * LANES + e]
        count = plan_ref[i * 2 * LANES + LANES + e]
        n_blocks = lax.div(count + (EXPERT_ROWS - 1), EXPERT_ROWS)

        def block(j, c2):
            base = (first + j * EXPERT_ROWS) * SLAB_ROWS
            old = [xf[pl.ds(base + q, EXPERT_ROWS, stride=SLAB_ROWS), :] for q in range(n_col)]
            xb = jnp.concatenate([_bf16(o) for o in old], axis=1)
            y = _swiglu(xb, w13_buf[slot], w2_buf[slot])
            live = (lax.broadcasted_iota(jnp.int32, (EXPERT_ROWS, LANES), 0)
                    < count - j * EXPERT_ROWS)
            for q in range(n_col):
                xf[pl.ds(base + q, EXPERT_ROWS, stride=SLAB_ROWS), :] = jnp.where(
                    live, y[:, q * LANES:(q + 1) * LANES], old[q])
            return c2

        lax.fori_loop(0, n_blocks, block, 0)
        return carry

    lax.fori_loop(0, n_exp, expert, 0)

    def gather(t, carry):
        acc = jnp.zeros((SLAB_ROWS, LANES), jnp.float32)
        for k in range(TOP_K):
            src = pl.multiple_of(slot_ref[t * TOP_K + k], SLAB_ROWS)
            acc = acc + wt_ref[t * TOP_K + k] * xf[pl.ds(src, SLAB_ROWS), :]
        rows_f[pl.ds(pl.multiple_of(t * SLAB_ROWS, SLAB_ROWS), SLAB_ROWS), :] = acc
        return carry

    lax.fori_loop(0, tm, gather, 0)
    def finish(b, carry):
        r0 = pl.multiple_of(b * ROW_BLOCK, ROW_BLOCK)
        routed = jnp.concatenate(
            [rows_f[pl.ds(r0 * SLAB_ROWS + j, ROW_BLOCK, stride=SLAB_ROWS), :] for j in range(n_col)],
            axis=1)
        x_blk, u = modulate(r0)
        ffn = routed + _swiglu(_bf16(u), ws13_ref[...], ws2_ref[...])
        res = ln_ref[2:3, :] * x_blk + (1.0 + g_ref[0]) * ffn
        o_ref[pl.ds(r0, ROW_BLOCK), :] = _layer_norm(res, ln_ref[0:1, :], ln_ref[1:2, :])
        return carry

    lax.fori_loop(0, tm // ROW_BLOCK, finish, 0)


def _expert_layer(x, sh, sc, g, plan, slots, wts, layer, w13, w2, ws13, ws2, ln, tm):
    bsz, s, d = x.shape
    per_b = s // tm
    n_tiles = bsz * per_b
    mod_spec = pl.BlockSpec((1, 1, d), lambda i, plan: (i // per_b, 0, 0))
    full2 = lambda i, plan: (0, 0)
    once = pl.Buffered(1)
    tile_smem = pl.BlockSpec((TOP_K * tm,), lambda i, plan: (i,), memory_space=pltpu.SMEM)
    slab_rows = (TOP_K * tm + EXPERT_ROWS) * SLAB_ROWS
    grid_spec = pltpu.PrefetchScalarGridSpec(
        num_scalar_prefetch=1,
        grid=(n_tiles,),
        in_specs=[
            tile_smem, tile_smem,
            pl.BlockSpec((tm, d), lambda i, plan: (i, 0), pipeline_mode=once),
            mod_spec, mod_spec, mod_spec,
            pl.BlockSpec(memory_space=pl.ANY),
            pl.BlockSpec(memory_space=pl.ANY),
            pl.BlockSpec(ws13.shape, full2, pipeline_mode=once),
            pl.BlockSpec(ws2.shape, full2, pipeline_mode=once),
            pl.BlockSpec(ln.shape, full2),
        ],
        out_specs=pl.BlockSpec((tm, d), lambda i, plan: (i, 0), pipeline_mode=once),
        scratch_shapes=[
            pltpu.VMEM((tm * SLAB_ROWS, LANES), jnp.float32),
            pltpu.VMEM((slab_rows, LANES), jnp.float32),
            pltpu.VMEM((W_SLOTS,) + w13.shape[2:], w13.dtype),
            pltpu.VMEM((W_SLOTS,) + w2.shape[2:], w2.dtype),
            pltpu.SemaphoreType.DMA((W_COPIES, W_SLOTS)),
        ],
    )
    out = pl.pallas_call(
        functools.partial(_expert_kernel, layer),
        grid_spec=grid_spec,
        out_shape=jax.ShapeDtypeStruct((bsz * s, d), jnp.float32),
        compiler_params=pltpu.CompilerParams(
            dimension_semantics=("arbitrary",),
            vmem_limit_bytes=VMEM_LIMIT_BYTES),
        name="moe_experts",
    )(plan.reshape(-1), slots[:, :, :TOP_K].reshape(-1), wts[:, :, :TOP_K].reshape(-1),
      x.reshape(bsz * s, d), sh, sc, g,
      w13, w2, ws13, ws2, ln)
    return out.reshape(bsz, s, d)


def _block_diag(w):
    h, n, _ = w.shape
    eye = jnp.eye(h, dtype=w.dtype)
    return (eye[:, None, :, None] * w[:, :, None, :]).reshape(h * n, h * n)


def _pad_rows(a, n_rows):
    return jnp.concatenate([a, jnp.zeros((n_rows - a.shape[0], a.shape[1]), a.dtype)], axis=0)


def _tile_rows(seq_len, want):
    t = min(want, seq_len)
    assert seq_len % t == 0 and t % RWKV_CHUNK == 0
    return t


def kernel(x, c, w_mod, b_mod, w_in, w_out, conv_a, conv_a_bias, ln_a_g, ln_a_b, conv_b, conv_b_bias, w_rg, b_rg, w_ig, b_ig, lru_lambda, mu_c, w0, w_w2, a0, w_a2, w_g2, k_k, k_a, r_k, gn_g, gn_b, conv_d, ln1_g, ln1_b, w_router, e_bias, w13, w2, ws13, ws2, ln2_g, ln2_b):
    n_layers = w_mod.shape[0]
    bsz, s, d = x.shape
    alpha_dn = (2.0 * n_layers) ** 0.25
    ts_mix = _tile_rows(s, 512)
    tm_moe = _tile_rows(s, 1024)
    bf = jnp.bfloat16

    mod = _modulation(c, w_mod, b_mod)
    bd = _block_diag(jnp.ones((N_HEADS, HEAD_DIM, HEAD_DIM), bf))
    alpha_row = jnp.full((1, d), alpha_dn, jnp.float32)
    tri_tok = jnp.triu(jnp.ones((tm_moe, tm_moe), bf), k=1)
    tri_exp = jnp.tril(jnp.ones((N_EXPERTS, N_EXPERTS), bf), k=-1)
    w13_bf, w2_bf = w13.astype(bf), w2.astype(bf)

    for l in range(n_layers):
        sh1, sc1, g1, sh2, sc2, g2 = [mod[l, :, i * d:(i + 1) * d].reshape(bsz, 1, d)
                                      for i in range(6)]
        cw = _pad_rows(jnp.concatenate([conv_a[l], conv_b[l], conv_d[l]], axis=0), N_CW_ROWS)
        vec = _pad_rows(jnp.stack([
            conv_a_bias[l], ln_a_g[l], ln_a_b[l], conv_b_bias[l], b_rg[l], b_ig[l], lru_lambda[l],
            w0[l], a0[l], k_k[l], k_a[l], r_k[l].reshape(D_GRP), gn_g[l], gn_b[l]], axis=0),
            N_VEC_ROWS)
        wgate = jnp.concatenate([_block_diag(w_rg[l]), _block_diag(w_ig[l])], axis=1).astype(bf)
        wlora = jnp.zeros((LORA_W + LORA_A + LORA_G, 3 * D_GRP), jnp.float32)
        wlora = wlora.at[0:LORA_W, 0:D_GRP].set(w_w2[l])
        wlora = wlora.at[LORA_W:LORA_W + LORA_A, D_GRP:2 * D_GRP].set(w_a2[l])
        wlora = wlora.at[LORA_W + LORA_A:, 2 * D_GRP:].set(w_g2[l]).astype(bf)
        ln1 = jnp.concatenate([ln1_g[l][None], ln1_b[l][None], alpha_row], axis=0)
        ln2 = jnp.concatenate([ln2_g[l][None], ln2_b[l][None], alpha_row], axis=0)

        x = _token_mixer_layer(x, sh1, sc1, g1, w_in[l].astype(bf), w_out[l].astype(bf), cw, vec,
                               mu_c[l][None], ln1, wgate, wlora, bd, ts_mix)
        slots, wts, plan = _router_layer(x, sh2, sc2, w_router[l].T, e_bias[l][:, None],
                                         tri_tok, tri_exp, tm_moe)
        x = _expert_layer(x, sh2, sc2, g2, plan, slots, wts, l, w13_bf, w2_bf,
                          ws13[l].astype(bf), ws2[l].astype(bf), ln2, tm_moe)
    return x
```

```python
import functools

import jax
import jax.numpy as jnp
from jax import lax
from jax.experimental import pallas as pl
from jax.experimental.pallas import tpu as pltpu

D_MODEL = 1024
HEAD_DIM = 64
D_GRP = 256
N_HEADS = D_GRP // HEAD_DIM
CONF_KERNEL = 31
LRU_CONV = 4
LRU_C = 8.0
SHORT_CONV = 3
LORA_W, LORA_A, LORA_G = 32, 32, 64
C_COLS = 3 * D_GRP + LORA_W + LORA_A + LORA_G
P_IN = 4 * D_GRP + C_COLS + 3 * D_GRP
RWKV_GN_EPS = 64e-5
N_EXPERTS = 64
TOP_K = 8
N_GROUPS = 8
TOPK_GROUPS = 4
E_PER_GROUP = N_EXPERTS // N_GROUPS
ROUTED_SCALE = 2.5
LN_EPS = 1e-5

OFF_A = 0
OFF_B = 2 * D_GRP
OFF_C = 4 * D_GRP
OFF_D = OFF_C + C_COLS

SUBLANES = 8
LANES = 128
VMEM_LIMIT_BYTES = 58 * 1024 * 1024

SLAB_ROWS = D_MODEL // LANES
EXPERT_ROWS = 144
W_SLOTS = 4
W_AHEAD = W_SLOTS - 1
RWKV_CHUNK = 64
CONF_HIST = 32
SMALL_HIST = 8

(V_CONV_A_BIAS, V_LN_A_G, V_LN_A_B, V_CONV_B_BIAS, V_B_RG, V_B_IG, V_LRU_LAMBDA, V_W0, V_A0,
 V_K_K, V_K_A, V_R_K, V_GN_G, V_GN_B) = range(14)
N_VEC_ROWS = 16
CW_A, CW_B, CW_D = 0, CONF_KERNEL, CONF_KERNEL + LRU_CONV
N_CW_ROWS = 40


def _bf16(x):
    return x.astype(jnp.bfloat16)


def _split(a):
    hi = _bf16(a)
    lo = _bf16(a - hi.astype(jnp.float32))
    return hi, lo


def _dot_f32(a, b, dims=((1,), (0,))):
    a_hi, a_lo = _split(a)
    b_hi, b_lo = _split(b)
    dn = (dims, ((), ()))
    f = functools.partial(lax.dot_general, dimension_numbers=dn,
                          preferred_element_type=jnp.float32)
    return f(a_hi, b_hi) + (f(a_hi, b_lo) + f(a_lo, b_hi))


def _dot_lhs_f32(a, b_bf16):
    a_hi, a_lo = _split(a)
    f = functools.partial(jnp.dot, preferred_element_type=jnp.float32)
    return f(a_hi, b_bf16) + f(a_lo, b_bf16)


def _layer_norm(x, g, b):
    mu = jnp.mean(x, axis=-1, keepdims=True)
    xc = x - mu
    var = jnp.mean(xc * xc, axis=-1, keepdims=True)
    return xc * lax.rsqrt(var + LN_EPS) * g + b


def _softplus(x):
    return jnp.maximum(x, 0.0) + jnp.log1p(jnp.exp(-jnp.abs(x)))


def _row_iota(shape):
    return lax.broadcasted_iota(jnp.int32, shape, 0)


def _shift_rows(x, d, fill):
    rolled = pltpu.roll(x, d, axis=0)
    return jnp.where(_row_iota(x.shape) >= d, rolled, fill)


def _mod_kernel(c_ref, w_ref, b_ref, o_ref):
    c = c_ref[...]
    c_act = c * jax.nn.sigmoid(c)
    o_ref[0] = _dot_f32(c_act, w_ref[0]) + b_ref[0]


def _modulation(c, w_mod, b_mod):
    n_layers, d, d6 = w_mod.shape
    bsz = c.shape[0]
    tn = D_MODEL
    return pl.pallas_call(
        _mod_kernel,
        grid=(n_layers, d6 // tn),
        in_specs=[
            pl.BlockSpec((bsz, d), lambda l, j: (0, 0)),
            pl.BlockSpec((1, d, tn), lambda l, j: (l, 0, j)),
            pl.BlockSpec((1, 1, tn), lambda l, j: (l, 0, j)),
        ],
        out_specs=pl.BlockSpec((1, bsz, tn), lambda l, j: (l, 0, j)),
        out_shape=jax.ShapeDtypeStruct((n_layers, bsz, d6), jnp.float32),
        compiler_params=pltpu.CompilerParams(
            dimension_semantics=("arbitrary", "arbitrary"),
            vmem_limit_bytes=VMEM_LIMIT_BYTES),
        name="adaln_modulation",
    )(c, w_mod, b_mod.reshape(n_layers, 1, d6))


def _causal_conv(buf_ref, hist, ts, w_rows, n_taps):
    acc = None
    for k in range(n_taps):
        off = hist - (n_taps - 1) + k
        term = w_rows[k:k + 1, :] * buf_ref[pl.ds(off, ts), :]
        acc = term if acc is None else acc + term
    return acc


def _causal_conv_long(buf_ref, hist, ts, w_rows, n_taps):
    ext = ts + SUBLANES
    out = None
    for r in range(SUBLANES):
        part = None
        for q in range((n_taps - 1 - r) // SUBLANES + 1):
            lag = SUBLANES * q + r
            term = (w_rows[n_taps - 1 - lag:n_taps - lag, :]
                    * buf_ref[pl.ds(hist - SUBLANES - SUBLANES * q, ext), :])
            part = term if part is None else part + term
        shifted = part if r == 0 else pltpu.roll(part, r, axis=0)
        piece = shifted[SUBLANES:, :]
        out = piece if out is None else out + piece
    return out


def _head_sum(x, bd_ref):
    return _dot_lhs_f32(x, bd_ref[...])


_NN = ((2,), (1,))
_NT = ((2,), (2,))
_TN = ((1,), (1,))


def _bdot(a, b, dims):
    return lax.dot_general(_bf16(a), _bf16(b), (dims, ((0,), (0,))),
                           preferred_element_type=jnp.float32)


def _to_problems(x):
    n_chunks = x.shape[0] // RWKV_CHUNK
    return jnp.stack([x[c * RWKV_CHUNK:(c + 1) * RWKV_CHUNK, h * HEAD_DIM:(h + 1) * HEAD_DIM]
                      for c in range(n_chunks) for h in range(N_HEADS)], axis=0)


def _rwkv_chunked(kk, r, alpha, kf, v, lw, state):
    ts = kk.shape[0]
    n_chunks = ts // RWKV_CHUNK
    cl = lw
    seg_row = _row_iota(cl.shape) % RWKV_CHUNK
    d = 1
    while d < RWKV_CHUNK:
        cl = cl + jnp.where(seg_row >= d, pltpu.roll(cl, d, axis=0), 0.0)
        d *= 2
    cl_end = cl.reshape(n_chunks, RWKV_CHUNK, D_GRP)[:, RWKV_CHUNK - 1:RWKV_CHUNK, :]
    cl_end_rows = jnp.broadcast_to(cl_end, (n_chunks, RWKV_CHUNK, D_GRP)).reshape(ts, D_GRP)
    g_inv = jnp.exp(-cl)
    tail = jnp.exp(cl_end_rows - cl)
    g_end = jnp.exp(cl_end)
    bt = _to_problems(_bf16(kk * jnp.exp(cl - lw)))
    rt = _to_problems(_bf16(r * jnp.exp(cl)))
    at = _to_problems(_bf16(alpha * g_inv))
    kt = _to_problems(_bf16(kf * g_inv))
    ab = _to_problems(_bf16(alpha * tail))
    kb = _to_problems(_bf16(kf * tail))
    vp = _to_problems(_bf16(v))

    c_sz = RWKV_CHUNK
    tri_r = lax.broadcasted_iota(jnp.int32, (1, c_sz, c_sz), 1)
    tri_c = lax.broadcasted_iota(jnp.int32, (1, c_sz, c_sz), 2)
    strict = tri_r > tri_c
    incl = tri_r >= tri_c
    eye = (tri_r == tri_c).astype(jnp.float32)

    amat = _bdot(jnp.concatenate([bt, rt], axis=1), jnp.concatenate([at, kt], axis=1), _NT)
    a_ba = jnp.where(strict, amat[:, :c_sz, :c_sz], 0.0)
    a_bk = jnp.where(strict, amat[:, :c_sz, c_sz:], 0.0)
    a_ra = jnp.where(incl, amat[:, c_sz:, :c_sz], 0.0)
    a_rk = jnp.where(incl, amat[:, c_sz:, c_sz:], 0.0)
    tinv = eye + a_ba
    pw = a_ba
    step = 2
    while step < c_sz:
        pw = _bdot(pw, pw, _NN)
        tinv = tinv + _bdot(pw, tinv, _NN)
        step *= 2
    tx = _bdot(tinv, jnp.concatenate([bt, _bf16(_bdot(a_bk, vp, _NN))], axis=2), _NN)
    w_t, u_t = tx[:, :, :HEAD_DIM], tx[:, :, HEAD_DIM:]
    y_k = _bdot(a_rk, vp, _NN)

    s_cur = state[...]
    y_rows = []
    for c in range(n_chunks):
        p = slice(c * N_HEADS, (c + 1) * N_HEADS)
        res = _bdot(jnp.concatenate([_bf16(w_t[p]), rt[p]], axis=1), s_cur, _NT)
        u_c = res[:, :c_sz, :] + u_t[p]
        y_c = res[:, c_sz:, :] + _bdot(a_ra[p], u_c, _NN) + y_k[p]
        g_c = jnp.stack([g_end[c, :, h * HEAD_DIM:(h + 1) * HEAD_DIM] for h in range(N_HEADS)],
                        axis=0)
        s_cur = s_cur * g_c + _bdot(jnp.concatenate([_bf16(u_c), vp[p]], axis=1),
                                    jnp.concatenate([ab[p], kb[p]], axis=1), _TN)
        y_rows.append(jnp.concatenate([y_c[h] for h in range(N_HEADS)], axis=1))
    state[...] = s_cur
    return jnp.concatenate(y_rows, axis=0)


def _mixer_kernel(x_ref, sh_ref, sc_ref, g_ref, w_in_ref, w_out_ref, cw_ref, vec_ref, mu_ref,
                  ln_ref, wgate_ref, wlora_ref, bd_ref, o_ref,
                  hbuf, xbbuf, zbuf, pc_last, lru_h, state, ymix):
    ts = x_ref.shape[1]
    s_idx = pl.program_id(1)

    @pl.when(s_idx == 0)
    def _reset():
        hbuf[pl.ds(0, CONF_HIST), :] = jnp.zeros((CONF_HIST, D_GRP), jnp.float32)
        xbbuf[pl.ds(0, SMALL_HIST), :] = jnp.zeros((SMALL_HIST, D_GRP), jnp.float32)
        zbuf[pl.ds(0, SMALL_HIST), :] = jnp.zeros((SMALL_HIST, D_GRP), jnp.float32)
        pc_last[...] = jnp.zeros(pc_last.shape, jnp.float32)
        lru_h[...] = jnp.zeros(lru_h.shape, jnp.float32)
        state[...] = jnp.zeros(state.shape, jnp.float32)

    x = x_ref[0]
    u = x * (1.0 + sc_ref[0]) + sh_ref[0]
    proj = jnp.dot(_bf16(u), w_in_ref[...], preferred_element_type=jnp.float32)

    vec = vec_ref[...]

    def vrow(i):
        return vec[i:i + 1, :]

    cw = cw_ref[...]

    val = proj[:, OFF_A:OFF_A + D_GRP]
    gate = proj[:, OFF_A + D_GRP:OFF_A + 2 * D_GRP]
    hbuf[pl.ds(CONF_HIST, ts), :] = val * jax.nn.sigmoid(gate)
    conv = _causal_conv_long(hbuf, CONF_HIST, ts, cw[CW_A:CW_A + CONF_KERNEL], CONF_KERNEL)
    conv = conv + vrow(V_CONV_A_BIAS)
    hbuf[pl.ds(0, CONF_HIST), :] = hbuf[pl.ds(ts, CONF_HIST), :]
    ln_a = _layer_norm(conv, vrow(V_LN_A_G), vrow(V_LN_A_B))
    ymix[:, 0:D_GRP] = ln_a * jax.nn.sigmoid(ln_a)

    xbbuf[pl.ds(SMALL_HIST, ts), :] = proj[:, OFF_B:OFF_B + D_GRP]
    gb = proj[:, OFF_B + D_GRP:OFF_B + 2 * D_GRP]
    ub = _causal_conv(xbbuf, SMALL_HIST, ts, cw[CW_B:CW_B + LRU_CONV], LRU_CONV)
    ub = ub + vrow(V_CONV_B_BIAS)
    xbbuf[pl.ds(0, SMALL_HIST), :] = xbbuf[pl.ds(ts, SMALL_HIST), :]
    gates = jnp.dot(_bf16(ub), wgate_ref[...], preferred_element_type=jnp.float32)
    r_gate = jax.nn.sigmoid(gates[:, :D_GRP] + vrow(V_B_RG))
    i_gate = jax.nn.sigmoid(gates[:, D_GRP:] + vrow(V_B_IG))
    log_a = (-LRU_C) * r_gate * _softplus(-vrow(V_LRU_LAMBDA))
    a_sc = jnp.exp(log_a)
    b_sc = jnp.sqrt(-jnp.tanh(log_a) * (a_sc * a_sc + 1.0)) * (i_gate * ub)
    d = 1
    while d < ts:
        a_sh = _shift_rows(a_sc, d, 1.0)
        b_sh = _shift_rows(b_sc, d, 0.0)
        b_sc = a_sc * b_sh + b_sc
        a_sc = a_sc * a_sh
        d *= 2
    h = b_sc + a_sc * lru_h[0:1, :]
    lru_h[0:1, :] = h[ts - 1:ts, :]
    ymix[:, D_GRP:2 * D_GRP] = h * jax.nn.gelu(gb, approximate=True)

    gbd = proj[:, OFF_D:OFF_D + D_GRP]
    zbuf[pl.ds(SMALL_HIST, ts), :] = (proj[:, OFF_D + D_GRP:OFF_D + 2 * D_GRP]
                                      * proj[:, OFF_D + 2 * D_GRP:OFF_D + 3 * D_GRP])
    convd = _causal_conv(zbuf, SMALL_HIST, ts, cw[CW_D:CW_D + SHORT_CONV], SHORT_CONV)
    zbuf[pl.ds(0, SMALL_HIST), :] = zbuf[pl.ds(ts, SMALL_HIST), :]
    ymix[:, 3 * D_GRP:4 * D_GRP] = gbd * convd

    pc = proj[:, OFF_C:OFF_C + C_COLS]
    prev = jnp.where(_row_iota(pc.shape) == 0, pc_last[0:1, :], pltpu.roll(pc, 1, axis=0))
    pc_last[0:1, :] = pc[ts - 1:ts, :]
    xs = pc + (prev - pc) * mu_ref[...]
    r = xs[:, 0:D_GRP]
    k = xs[:, D_GRP:2 * D_GRP]
    v = xs[:, 2 * D_GRP:3 * D_GRP]
    z = xs[:, 3 * D_GRP:C_COLS]
    lane = lax.broadcasted_iota(jnp.int32, z.shape, 1)
    z_act = jnp.where(lane < LORA_W, jnp.tanh(z),
                      jnp.where(lane < LORA_W + LORA_A, z, jax.nn.sigmoid(z)))
    lora = jnp.dot(_bf16(z_act), wlora_ref[...], preferred_element_type=jnp.float32)
    wlog = -_softplus(-(vrow(V_W0) + lora[:, 0:D_GRP])) - 0.5
    lw = -jnp.exp(wlog)
    a_icl = jax.nn.sigmoid(vrow(V_A0) + lora[:, D_GRP:2 * D_GRP])
    g_out = lora[:, 2 * D_GRP:3 * D_GRP]

    kk = k * vrow(V_K_K)
    kk = kk * lax.rsqrt(_head_sum(kk * kk, bd_ref) + 1e-12)
    kf = k * (1.0 + (a_icl - 1.0) * vrow(V_K_A))

    y = _rwkv_chunked(kk, r, -(kk * a_icl), kf, v, lw, state)
    inv_n = 1.0 / HEAD_DIM
    mu_y = _head_sum(y, bd_ref) * inv_n
    yc = y - mu_y
    var_y = _head_sum(yc * yc, bd_ref) * inv_n
    y = yc * lax.rsqrt(var_y + RWKV_GN_EPS) * vrow(V_GN_G) + vrow(V_GN_B)
    bonus = _head_sum(r * kf * vrow(V_R_K), bd_ref) * v
    ymix[:, 2 * D_GRP:3 * D_GRP] = (y + bonus) * g_out

    mix = jnp.dot(_bf16(ymix[...]), w_out_ref[...], preferred_element_type=jnp.float32)
    alpha_dn = ln_ref[2:3, :]
    res = alpha_dn * x + (1.0 + g_ref[0]) * mix
    o_ref[0] = _layer_norm(res, ln_ref[0:1, :], ln_ref[1:2, :])


def _token_mixer_layer(x, sh, sc, g, w_in, w_out, cw, vec, mu, ln, wgate, wlora, bd, ts):
    bsz, s, d = x.shape
    grid = (bsz, s // ts)
    full2 = lambda b, i: (0, 0)
    mod_spec = pl.BlockSpec((1, 1, d), lambda b, i: (b, 0, 0))
    f32 = jnp.float32

    def tile(n_cols):
        return pltpu.VMEM((ts, n_cols), f32)

    return pl.pallas_call(
        _mixer_kernel,
        grid=grid,
        in_specs=[
            pl.BlockSpec((1, ts, d), lambda b, i: (b, i, 0)),
            mod_spec, mod_spec, mod_spec,
            pl.BlockSpec(w_in.shape, full2),
            pl.BlockSpec(w_out.shape, full2),
            pl.BlockSpec(cw.shape, full2),
            pl.BlockSpec(vec.shape, full2),
            pl.BlockSpec(mu.shape, full2),
            pl.BlockSpec(ln.shape, full2),
            pl.BlockSpec(wgate.shape, full2),
            pl.BlockSpec(wlora.shape, full2),
            pl.BlockSpec(bd.shape, full2),
        ],
        out_specs=pl.BlockSpec((1, ts, d), lambda b, i: (b, i, 0)),
        out_shape=jax.ShapeDtypeStruct((bsz, s, d), f32),
        scratch_shapes=[
            pltpu.VMEM((CONF_HIST + ts, D_GRP), f32),
            pltpu.VMEM((SMALL_HIST + ts, D_GRP), f32),
            pltpu.VMEM((SMALL_HIST + ts, D_GRP), f32),
            pltpu.VMEM((SUBLANES, C_COLS), f32),
            pltpu.VMEM((SUBLANES, D_GRP), f32),
            pltpu.VMEM((N_HEADS, HEAD_DIM, HEAD_DIM), f32),
            tile(D_MODEL),
        ],
        compiler_params=pltpu.CompilerParams(
            dimension_semantics=("arbitrary", "arbitrary"),
            vmem_limit_bytes=VMEM_LIMIT_BYTES),
        name="token_mixers",
    )(x, sh, sc, g, w_in, w_out, cw, vec, mu, ln, wgate, wlora, bd)


def _first_index(mask, idx, sentinel):
    return jnp.min(jnp.where(mask, idx, sentinel), axis=0, keepdims=True)


def _router_kernel(x_ref, sh_ref, sc_ref, wr_ref, eb_ref, su_ref, sl_ref,
                   slot_ref, wt_ref, plan_ref):
    x = x_ref[0]
    u = x * (1.0 + sc_ref[0]) + sh_ref[0]
    tm = u.shape[0]
    logits = _dot_f32(wr_ref[...], u, ((1,), (1,)))
    scores = jax.nn.sigmoid(logits)
    biased = scores + eb_ref[...]
    neg_inf = jnp.float32(-jnp.inf)
    sub = lax.broadcasted_iota(jnp.int32, (E_PER_GROUP, tm), 0).astype(jnp.float32)

    groups = [biased[g * E_PER_GROUP:(g + 1) * E_PER_GROUP, :] for g in range(N_GROUPS)]
    gscore = []
    for blk in groups:
        m1 = jnp.max(blk, axis=0, keepdims=True)
        first = _first_index(blk == m1, sub, float(E_PER_GROUP))
        m2 = jnp.max(jnp.where(sub == first, neg_inf, blk), axis=0, keepdims=True)
        gscore.append(m1 + m2)
    gs = jnp.concatenate(gscore, axis=0)
    gidx = lax.broadcasted_iota(jnp.int32, (N_GROUPS, tm), 0).astype(jnp.float32)
    keep = jnp.zeros((N_GROUPS, tm), jnp.float32)
    for _ in range(TOPK_GROUPS):
        m = jnp.max(gs, axis=0, keepdims=True)
        first = _first_index(gs == m, gidx, float(N_GROUPS))
        sel = gidx == first
        keep = jnp.where(sel, 1.0, keep)
        gs = jnp.where(sel, neg_inf, gs)

    masked = [jnp.where(keep[g:g + 1, :] > 0.5, groups[g], neg_inf) for g in range(N_GROUPS)]
    eidx = [sub + float(g * E_PER_GROUP) for g in range(N_GROUPS)]
    chosen = [jnp.zeros((E_PER_GROUP, tm), jnp.float32) for _ in range(N_GROUPS)]
    for _ in range(TOP_K):
        m = functools.reduce(jnp.maximum, [jnp.max(b, axis=0, keepdims=True) for b in masked])
        first = functools.reduce(
            jnp.minimum,
            [_first_index(masked[g] == m, eidx[g], float(N_EXPERTS)) for g in range(N_GROUPS)])
        for g in range(N_GROUPS):
            sel = eidx[g] == first
            chosen[g] = jnp.where(sel, 1.0, chosen[g])
            masked[g] = jnp.where(sel, neg_inf, masked[g])

    picked = [jnp.where(chosen[g] > 0.5, scores[g * E_PER_GROUP:(g + 1) * E_PER_GROUP, :], 0.0)
              for g in range(N_GROUPS)]
    total = functools.reduce(jnp.add, [jnp.sum(p, axis=0, keepdims=True) for p in picked])
    wts = jnp.concatenate([p / total * ROUTED_SCALE for p in picked], axis=0)

    sel = jnp.concatenate(chosen, axis=0)
    sel_bf = _bf16(sel)
    rank = jnp.dot(sel_bf, su_ref[...], preferred_element_type=jnp.float32)
    cnt = jnp.sum(sel, axis=1, keepdims=True)
    off = _dot_f32(sl_ref[...].astype(jnp.float32), jnp.broadcast_to(cnt, (N_EXPERTS, LANES)))
    slot = off[:, 0:1] + rank
    kidx = jnp.dot(sl_ref[...], sel_bf, preferred_element_type=jnp.float32)
    slot_rows, wt_rows = [], []
    for kk in range(TOP_K):
        pick = (sel > 0.5) & (kidx == float(kk))
        slot_rows.append(jnp.sum(jnp.where(pick, slot, 0.0), axis=0, keepdims=True))
        wt_rows.append(jnp.sum(jnp.where(pick, wts, 0.0), axis=0, keepdims=True))
    lane_pad = [jnp.zeros((LANES - TOP_K, tm), jnp.float32)]
    slot_ref[0] = (jnp.concatenate(slot_rows + lane_pad, axis=0).T
                   * float(SLAB_ROWS)).astype(jnp.int32)
    wt_ref[0] = jnp.concatenate(wt_rows + lane_pad, axis=0).T
    e_sub = lax.broadcasted_iota(jnp.int32, (N_EXPERTS, LANES), 0)
    e_lane = lax.broadcasted_iota(jnp.int32, (N_EXPERTS, LANES), 1)
    diag = e_sub == e_lane
    off_row = jnp.sum(jnp.where(diag, off, 0.0), axis=0, keepdims=True)
    cnt_row = jnp.sum(jnp.where(diag, jnp.broadcast_to(cnt, (N_EXPERTS, LANES)), 0.0),
                      axis=0, keepdims=True)
    plan_ref[0] = jnp.concatenate([off_row, cnt_row], axis=0).astype(jnp.int32)


def _router_layer(x, sh, sc, w_router_t, e_bias_col, su, sl, tm):
    bsz, s, d = x.shape
    per_b = s // tm
    n_tiles = bsz * per_b
    mod_spec = pl.BlockSpec((1, 1, d), lambda b, i: (b, 0, 0))
    const2 = lambda b, i: (0, 0)
    tile3 = lambda b, i: (b * per_b + i, 0, 0)
    return pl.pallas_call(
        _router_kernel,
        grid=(bsz, per_b),
        in_specs=[
            pl.BlockSpec((1, tm, d), lambda b, i: (b, i, 0)),
            mod_spec, mod_spec,
            pl.BlockSpec(w_router_t.shape, const2),
            pl.BlockSpec(e_bias_col.shape, const2),
            pl.BlockSpec(su.shape, const2),
            pl.BlockSpec(sl.shape, const2),
        ],
        out_specs=[
            pl.BlockSpec((1, tm, LANES), tile3),
            pl.BlockSpec((1, tm, LANES), tile3),
            pl.BlockSpec((1, 2, LANES), tile3),
        ],
        out_shape=[
            jax.ShapeDtypeStruct((n_tiles, tm, LANES), jnp.int32),
            jax.ShapeDtypeStruct((n_tiles, tm, LANES), jnp.float32),
            jax.ShapeDtypeStruct((n_tiles, 2, LANES), jnp.int32),
        ],
        compiler_params=pltpu.CompilerParams(
            dimension_semantics=("arbitrary", "arbitrary"),
            vmem_limit_bytes=VMEM_LIMIT_BYTES),
        name="moe_router",
    )(x, sh, sc, w_router_t, e_bias_col, su, sl)


def _swiglu(u_bf16, w13, w2):
    hcat = jnp.dot(u_bf16, w13, preferred_element_type=jnp.float32)
    half = hcat.shape[1] // 2
    gate, up = hcat[:, :half], hcat[:, half:]
    act = gate * jax.nn.sigmoid(gate) * up
    return jnp.dot(_bf16(act), w2, preferred_element_type=jnp.float32)


N_DMA_QUEUES = 2
W_COPIES = 2 * N_DMA_QUEUES


def _weight_copies(layer, w13_hbm, w2_hbm, w13_buf, w2_buf, sems, e, slot):
    copies = []
    for hbm, buf in ((w13_hbm, w13_buf), (w2_hbm, w2_buf)):
        rows = hbm.shape[2] // N_DMA_QUEUES
        for part in range(N_DMA_QUEUES):
            sl = pl.ds(part * rows, rows)
            copies.append(pltpu.make_async_copy(hbm.at[layer, e, sl], buf.at[slot, sl],
                                                sems.at[len(copies), slot]))
    return copies


def _start_weight_copies(layer, w13_hbm, w2_hbm, w13_buf, w2_buf, sems, e, slot):
    for n, cp in enumerate(_weight_copies(layer, w13_hbm, w2_hbm, w13_buf, w2_buf, sems, e, slot)):
        cp.start(priority=n % N_DMA_QUEUES)


def _expert_kernel(layer, plan_ref, slot_ref, wt_ref, x_ref, sh_ref, sc_ref, g_ref, w13_hbm,
                   w2_hbm, ws13_ref, ws2_ref, ln_ref, o_ref, rows_f, xf, w13_buf, w2_buf, sems):
    i = pl.program_id(0)
    tm = x_ref.shape[0]
    n_exp = w13_hbm.shape[1]
    n_col = x_ref.shape[1] // LANES

    @pl.when(i == 0)
    def _clear():
        xf[...] = jnp.zeros(xf.shape, xf.dtype)

    for e0 in range(W_AHEAD):
        _start_weight_copies(layer, w13_hbm, w2_hbm, w13_buf, w2_buf, sems, e0, e0)

    u = x_ref[...] * (1.0 + sc_ref[0]) + sh_ref[0]
    for j in range(n_col):
        rows_f[pl.ds(j, tm, stride=SLAB_ROWS), :] = u[:, j * LANES:(j + 1) * LANES]

    def scatter(t, carry):
        row = rows_f[pl.ds(pl.multiple_of(t * SLAB_ROWS, SLAB_ROWS), SLAB_ROWS), :]
        for k in range(TOP_K):
            dst = pl.multiple_of(slot_ref[t * TOP_K + k], SLAB_ROWS)
            xf[pl.ds(dst, SLAB_ROWS), :] = row
        return carry

    lax.fori_loop(0, tm, scatter, 0)

    def expert(e, carry):
        slot = lax.rem(e, W_SLOTS)
        for cp in _weight_copies(layer, w13_hbm, w2_hbm, w13_buf, w2_buf, sems, e, slot):
            cp.wait()

        @pl.when(e + W_AHEAD < n_exp)
        def _prefetch():
            nxt = e + W_AHEAD
            _start_weight_copies(layer, w13_hbm, w2_hbm, w13_buf, w2_buf, sems, nxt,
                                 lax.rem(nxt, W_SLOTS))

        first = plan_ref[i * 2 * LANES + e]
        count = plan_ref[i * 2 * LANES + LANES + e]
        n_blocks = lax.div(count + (EXPERT_ROWS - 1), EXPERT_ROWS)

        def block(j, c2):
            base = (first + j * EXPERT_ROWS) * SLAB_ROWS
            old = [xf[pl.ds(base + q, EXPERT_ROWS, stride=SLAB_ROWS), :] for q in range(n_col)]
            xb = jnp.concatenate([_bf16(o) for o in old], axis=1)
            y = _swiglu(xb, w13_buf[slot], w2_buf[slot])
            live = (lax.broadcasted_iota(jnp.int32, (EXPERT_ROWS, LANES), 0)
                    < count - j * EXPERT_ROWS)
            for q in range(n_col):
                xf[pl.ds(base + q, EXPERT_ROWS, stride=SLAB_ROWS), :] = jnp.where(
                    live, y[:, q * LANES:(q + 1) * LANES], old[q])
            return c2

        lax.fori_loop(0, n_blocks, block, 0)
        return carry

    lax.fori_loop(0, n_exp, expert, 0)

    def gather(t, carry):
        acc = jnp.zeros((SLAB_ROWS, LANES), jnp.float32)
        for k in range(TOP_K):
            src = pl.multiple_of(slot_ref[t * TOP_K + k], SLAB_ROWS)
            acc = acc + wt_ref[t * TOP_K + k] * xf[pl.ds(src, SLAB_ROWS), :]
        rows_f[pl.ds(pl.multiple_of(t * SLAB_ROWS, SLAB_ROWS), SLAB_ROWS), :] = acc
        return carry

    lax.fori_loop(0, tm, gather, 0)
    routed = jnp.concatenate([rows_f[pl.ds(j, tm, stride=SLAB_ROWS), :] for j in range(n_col)],
                             axis=1)
    x = x_ref[...]
    u = x * (1.0 + sc_ref[0]) + sh_ref[0]
    ffn = routed + _swiglu(_bf16(u), ws13_ref[...], ws2_ref[...])
    res = ln_ref[2:3, :] * x + (1.0 + g_ref[0]) * ffn
    o_ref[...] = _layer_norm(res, ln_ref[0:1, :], ln_ref[1:2, :])


def _expert_layer(x, sh, sc, g, plan, slots, wts, layer, w13, w2, ws13, ws2, ln, tm):
    bsz, s, d = x.shape
    per_b = s // tm
    n_tiles = bsz * per_b
    mod_spec = pl.BlockSpec((1, 1, d), lambda i, plan: (i // per_b, 0, 0))
    full2 = lambda i, plan: (0, 0)
    once = pl.Buffered(1)
    tile_smem = pl.BlockSpec((TOP_K * tm,), lambda i, plan: (i,), memory_space=pltpu.SMEM)
    slab_rows = (TOP_K * tm + EXPERT_ROWS) * SLAB_ROWS
    grid_spec = pltpu.PrefetchScalarGridSpec(
        num_scalar_prefetch=1,
        grid=(n_tiles,),
        in_specs=[
            tile_smem, tile_smem,
            pl.BlockSpec((tm, d), lambda i, plan: (i, 0), pipeline_mode=once),
            mod_spec, mod_spec, mod_spec,
            pl.BlockSpec(memory_space=pl.ANY),
            pl.BlockSpec(memory_space=pl.ANY),
            pl.BlockSpec(ws13.shape, full2, pipeline_mode=once),
            pl.BlockSpec(ws2.shape, full2, pipeline_mode=once),
            pl.BlockSpec(ln.shape, full2),
        ],
        out_specs=pl.BlockSpec((tm, d), lambda i, plan: (i, 0), pipeline_mode=once),
        scratch_shapes=[
            pltpu.VMEM((tm * SLAB_ROWS, LANES), jnp.float32),
            pltpu.VMEM((slab_rows, LANES), jnp.float32),
            pltpu.VMEM((W_SLOTS,) + w13.shape[2:], w13.dtype),
            pltpu.VMEM((W_SLOTS,) + w2.shape[2:], w2.dtype),
            pltpu.SemaphoreType.DMA((W_COPIES, W_SLOTS)),
        ],
    )
    out = pl.pallas_call(
        functools.partial(_expert_kernel, layer),
        grid_spec=grid_spec,
        out_shape=jax.ShapeDtypeStruct((bsz * s, d), jnp.float32),
        compiler_params=pltpu.CompilerParams(
            dimension_semantics=("arbitrary",),
            vmem_limit_bytes=VMEM_LIMIT_BYTES),
        name="moe_experts",
    )(plan.reshape(-1), slots[:, :, :TOP_K].reshape(-1), wts[:, :, :TOP_K].reshape(-1),
      x.reshape(bsz * s, d), sh, sc, g,
      w13, w2, ws13, ws2, ln)
    return out.reshape(bsz, s, d)


def _block_diag(w):
    h, n, _ = w.shape
    eye = jnp.eye(h, dtype=w.dtype)
    return (eye[:, None, :, None] * w[:, :, None, :]).reshape(h * n, h * n)


def _pad_rows(a, n_rows):
    return jnp.concatenate([a, jnp.zeros((n_rows - a.shape[0], a.shape[1]), a.dtype)], axis=0)


def _tile_rows(seq_len, want):
    t = min(want, seq_len)
    assert seq_len % t == 0 and t % RWKV_CHUNK == 0
    return t


def kernel(x, c, w_mod, b_mod, w_in, w_out, conv_a, conv_a_bias, ln_a_g, ln_a_b, conv_b, conv_b_bias, w_rg, b_rg, w_ig, b_ig, lru_lambda, mu_c, w0, w_w2, a0, w_a2, w_g2, k_k, k_a, r_k, gn_g, gn_b, conv_d, ln1_g, ln1_b, w_router, e_bias, w13, w2, ws13, ws2, ln2_g, ln2_b):
    n_layers = w_mod.shape[0]
    bsz, s, d = x.shape
    alpha_dn = (2.0 * n_layers) ** 0.25
    ts_mix = _tile_rows(s, 512)
    tm_moe = _tile_rows(s, 1024)
    bf = jnp.bfloat16

    mod = _modulation(c, w_mod, b_mod)
    bd = _block_diag(jnp.ones((N_HEADS, HEAD_DIM, HEAD_DIM), bf))
    alpha_row = jnp.full((1, d), alpha_dn, jnp.float32)
    tri_tok = jnp.triu(jnp.ones((tm_moe, tm_moe), bf), k=1)
    tri_exp = jnp.tril(jnp.ones((N_EXPERTS, N_EXPERTS), bf), k=-1)
    w13_bf, w2_bf = w13.astype(bf), w2.astype(bf)

    for l in range(n_layers):
        sh1, sc1, g1, sh2, sc2, g2 = [mod[l, :, i * d:(i + 1) * d].reshape(bsz, 1, d)
                                      for i in range(6)]
        cw = _pad_rows(jnp.concatenate([conv_a[l], conv_b[l], conv_d[l]], axis=0), N_CW_ROWS)
        vec = _pad_rows(jnp.stack([
            conv_a_bias[l], ln_a_g[l], ln_a_b[l], conv_b_bias[l], b_rg[l], b_ig[l], lru_lambda[l],
            w0[l], a0[l], k_k[l], k_a[l], r_k[l].reshape(D_GRP), gn_g[l], gn_b[l]], axis=0),
            N_VEC_ROWS)
        wgate = jnp.concatenate([_block_diag(w_rg[l]), _block_diag(w_ig[l])], axis=1).astype(bf)
        wlora = jnp.zeros((LORA_W + LORA_A + LORA_G, 3 * D_GRP), jnp.float32)
        wlora = wlora.at[0:LORA_W, 0:D_GRP].set(w_w2[l])
        wlora = wlora.at[LORA_W:LORA_W + LORA_A, D_GRP:2 * D_GRP].set(w_a2[l])
        wlora = wlora.at[LORA_W + LORA_A:, 2 * D_GRP:].set(w_g2[l]).astype(bf)
        ln1 = jnp.concatenate([ln1_g[l][None], ln1_b[l][None], alpha_row], axis=0)
        ln2 = jnp.concatenate([ln2_g[l][None], ln2_b[l][None], alpha_row], axis=0)

        x = _token_mixer_layer(x, sh1, sc1, g1, w_in[l].astype(bf), w_out[l].astype(bf), cw, vec,
                               mu_c[l][None], ln1, wgate, wlora, bd, ts_mix)
        slots, wts, plan = _router_layer(x, sh2, sc2, w_router[l].T, e_bias[l][:, None],
                                         tri_tok, tri_exp, tm_moe)
        x = _expert_layer(x, sh2, sc2, g2, plan, slots, wts, l, w13_bf, w2_bf,
                          ws13[l].astype(bf), ws2[l].astype(bf), ln2, tm_moe)
    return x
```

```python
import functools

import jax
import jax.numpy as jnp
from jax import lax
from jax.experimental import pallas as pl
from jax.experimental.pallas import tpu as pltpu

D_MODEL = 1024
HEAD_DIM = 64
D_GRP = 256
N_HEADS = D_GRP // HEAD_DIM
CONF_KERNEL = 31
LRU_CONV = 4
LRU_C = 8.0
SHORT_CONV = 3
LORA_W, LORA_A, LORA_G = 32, 32, 64
C_COLS = 3 * D_GRP + LORA_W + LORA_A + LORA_G
P_IN = 4 * D_GRP + C_COLS + 3 * D_GRP
RWKV_GN_EPS = 64e-5
N_EXPERTS = 64
TOP_K = 8
N_GROUPS = 8
TOPK_GROUPS = 4
E_PER_GROUP = N_EXPERTS // N_GROUPS
ROUTED_SCALE = 2.5
LN_EPS = 1e-5

OFF_A = 0
OFF_B = 2 * D_GRP
OFF_C = 4 * D_GRP
OFF_D = OFF_C + C_COLS

SUBLANES = 8
LANES = 128
VMEM_LIMIT_BYTES = 58 * 1024 * 1024

SLAB_ROWS = D_MODEL // LANES
EXPERT_ROWS = 144
TOKEN_UNROLL = 4
W_SLOTS = 4
W_AHEAD = W_SLOTS - 1
RWKV_CHUNK = 64
CONF_HIST = 32
SMALL_HIST = 8

(V_CONV_A_BIAS, V_LN_A_G, V_LN_A_B, V_CONV_B_BIAS, V_B_RG, V_B_IG, V_LRU_LAMBDA, V_W0, V_A0,
 V_K_K, V_K_A, V_R_K, V_GN_G, V_GN_B) = range(14)
N_VEC_ROWS = 16
CW_A, CW_B, CW_D = 0, CONF_KERNEL, CONF_KERNEL + LRU_CONV
N_CW_ROWS = 40


def _bf16(x):
    return x.astype(jnp.bfloat16)


def _split(a):
    hi = _bf16(a)
    lo = _bf16(a - hi.astype(jnp.float32))
    return hi, lo


def _dot_f32(a, b, dims=((1,), (0,))):
    a_hi, a_lo = _split(a)
    b_hi, b_lo = _split(b)
    dn = (dims, ((), ()))
    f = functools.partial(lax.dot_general, dimension_numbers=dn,
                          preferred_element_type=jnp.float32)
    return f(a_hi, b_hi) + (f(a_hi, b_lo) + f(a_lo, b_hi))


def _dot_lhs_f32(a, b_bf16):
    a_hi, a_lo = _split(a)
    f = functools.partial(jnp.dot, preferred_element_type=jnp.float32)
    return f(a_hi, b_bf16) + f(a_lo, b_bf16)


def _layer_norm(x, g, b):
    mu = jnp.mean(x, axis=-1, keepdims=True)
    xc = x - mu
    var = jnp.mean(xc * xc, axis=-1, keepdims=True)
    return xc * lax.rsqrt(var + LN_EPS) * g + b


def _softplus(x):
    return jnp.maximum(x, 0.0) + jnp.log1p(jnp.exp(-jnp.abs(x)))


def _row_iota(shape):
    return lax.broadcasted_iota(jnp.int32, shape, 0)


def _shift_rows(x, d, fill):
    rolled = pltpu.roll(x, d, axis=0)
    return jnp.where(_row_iota(x.shape) >= d, rolled, fill)


def _mod_kernel(c_ref, w_ref, b_ref, o_ref):
    c = c_ref[...]
    c_act = c * jax.nn.sigmoid(c)
    o_ref[0] = _dot_f32(c_act, w_ref[0]) + b_ref[0]


def _modulation(c, w_mod, b_mod):
    n_layers, d, d6 = w_mod.shape
    bsz = c.shape[0]
    tn = D_MODEL
    return pl.pallas_call(
        _mod_kernel,
        grid=(n_layers, d6 // tn),
        in_specs=[
            pl.BlockSpec((bsz, d), lambda l, j: (0, 0)),
            pl.BlockSpec((1, d, tn), lambda l, j: (l, 0, j)),
            pl.BlockSpec((1, 1, tn), lambda l, j: (l, 0, j)),
        ],
        out_specs=pl.BlockSpec((1, bsz, tn), lambda l, j: (l, 0, j)),
        out_shape=jax.ShapeDtypeStruct((n_layers, bsz, d6), jnp.float32),
        compiler_params=pltpu.CompilerParams(
            dimension_semantics=("arbitrary", "arbitrary"),
            vmem_limit_bytes=VMEM_LIMIT_BYTES),
        name="adaln_modulation",
    )(c, w_mod, b_mod.reshape(n_layers, 1, d6))


def _causal_conv(buf_ref, hist, ts, w_rows, n_taps):
    acc = None
    for k in range(n_taps):
        off = hist - (n_taps - 1) + k
        term = w_rows[k:k + 1, :] * buf_ref[pl.ds(off, ts), :]
        acc = term if acc is None else acc + term
    return acc


def _causal_conv_long(buf_ref, hist, ts, w_rows, n_taps):
    ext = ts + SUBLANES
    out = None
    for r in range(SUBLANES):
        part = None
        for q in range((n_taps - 1 - r) // SUBLANES + 1):
            lag = SUBLANES * q + r
            term = (w_rows[n_taps - 1 - lag:n_taps - lag, :]
                    * buf_ref[pl.ds(hist - SUBLANES - SUBLANES * q, ext), :])
            part = term if part is None else part + term
        shifted = part if r == 0 else pltpu.roll(part, r, axis=0)
        piece = shifted[SUBLANES:, :]
        out = piece if out is None else out + piece
    return out


def _head_sum(x, bd_ref):
    return _dot_lhs_f32(x, bd_ref[...])


_NN = ((2,), (1,))
_NT = ((2,), (2,))
_TN = ((1,), (1,))


def _bdot(a, b, dims):
    return lax.dot_general(_bf16(a), _bf16(b), (dims, ((0,), (0,))),
                           preferred_element_type=jnp.float32)


def _to_problems(x):
    n_chunks = x.shape[0] // RWKV_CHUNK
    return jnp.stack([x[c * RWKV_CHUNK:(c + 1) * RWKV_CHUNK, h * HEAD_DIM:(h + 1) * HEAD_DIM]
                      for c in range(n_chunks) for h in range(N_HEADS)], axis=0)


def _rwkv_chunked(kk, r, alpha, kf, v, lw, state):
    ts = kk.shape[0]
    n_chunks = ts // RWKV_CHUNK
    cl = lw
    seg_row = _row_iota(cl.shape) % RWKV_CHUNK
    d = 1
    while d < RWKV_CHUNK:
        cl = cl + jnp.where(seg_row >= d, pltpu.roll(cl, d, axis=0), 0.0)
        d *= 2
    cl_end = cl.reshape(n_chunks, RWKV_CHUNK, D_GRP)[:, RWKV_CHUNK - 1:RWKV_CHUNK, :]
    cl_end_rows = jnp.broadcast_to(cl_end, (n_chunks, RWKV_CHUNK, D_GRP)).reshape(ts, D_GRP)
    g_inv = jnp.exp(-cl)
    tail = jnp.exp(cl_end_rows - cl)
    g_end = jnp.exp(cl_end)
    bt = _to_problems(_bf16(kk * jnp.exp(cl - lw)))
    rt = _to_problems(_bf16(r * jnp.exp(cl)))
    at = _to_problems(_bf16(alpha * g_inv))
    kt = _to_problems(_bf16(kf * g_inv))
    ab = _to_problems(_bf16(alpha * tail))
    kb = _to_problems(_bf16(kf * tail))
    vp = _to_problems(_bf16(v))

    c_sz = RWKV_CHUNK
    tri_r = lax.broadcasted_iota(jnp.int32, (1, c_sz, c_sz), 1)
    tri_c = lax.broadcasted_iota(jnp.int32, (1, c_sz, c_sz), 2)
    strict = tri_r > tri_c
    incl = tri_r >= tri_c
    eye = (tri_r == tri_c).astype(jnp.float32)

    amat = _bdot(jnp.concatenate([bt, rt], axis=1), jnp.concatenate([at, kt], axis=1), _NT)
    a_ba = jnp.where(strict, amat[:, :c_sz, :c_sz], 0.0)
    a_bk = jnp.where(strict, amat[:, :c_sz, c_sz:], 0.0)
    a_ra = jnp.where(incl, amat[:, c_sz:, :c_sz], 0.0)
    a_rk = jnp.where(incl, amat[:, c_sz:, c_sz:], 0.0)
    tinv = eye + a_ba
    pw = a_ba
    step = 2
    while step < c_sz:
        pw = _bdot(pw, pw, _NN)
        tinv = tinv + _bdot(pw, tinv, _NN)
        step *= 2
    tx = _bdot(tinv, jnp.concatenate([bt, _bf16(_bdot(a_bk, vp, _NN))], axis=2), _NN)
    w_t, u_t = tx[:, :, :HEAD_DIM], tx[:, :, HEAD_DIM:]
    y_k = _bdot(a_rk, vp, _NN)

    s_cur = state[...]
    y_rows = []
    for c in range(n_chunks):
        p = slice(c * N_HEADS, (c + 1) * N_HEADS)
        res = _bdot(jnp.concatenate([_bf16(w_t[p]), rt[p]], axis=1), s_cur, _NT)
        u_c = res[:, :c_sz, :] + u_t[p]
        y_c = res[:, c_sz:, :] + _bdot(a_ra[p], u_c, _NN) + y_k[p]
        g_c = jnp.stack([g_end[c, :, h * HEAD_DIM:(h + 1) * HEAD_DIM] for h in range(N_HEADS)],
                        axis=0)
        s_cur = s_cur * g_c + _bdot(jnp.concatenate([_bf16(u_c), vp[p]], axis=1),
                                    jnp.concatenate([ab[p], kb[p]], axis=1), _TN)
        y_rows.append(jnp.concatenate([y_c[h] for h in range(N_HEADS)], axis=1))
    state[...] = s_cur
    return jnp.concatenate(y_rows, axis=0)


def _mixer_kernel(x_ref, sh_ref, sc_ref, g_ref, w_in_ref, w_out_ref, cw_ref, vec_ref, mu_ref,
                  ln_ref, wgate_ref, wlora_ref, bd_ref, o_ref,
                  hbuf, xbbuf, zbuf, pc_last, lru_h, state, ymix):
    ts = x_ref.shape[1]
    s_idx = pl.program_id(1)

    @pl.when(s_idx == 0)
    def _reset():
        hbuf[pl.ds(0, CONF_HIST), :] = jnp.zeros((CONF_HIST, D_GRP), jnp.float32)
        xbbuf[pl.ds(0, SMALL_HIST), :] = jnp.zeros((SMALL_HIST, D_GRP), jnp.float32)
        zbuf[pl.ds(0, SMALL_HIST), :] = jnp.zeros((SMALL_HIST, D_GRP), jnp.float32)
        pc_last[...] = jnp.zeros(pc_last.shape, jnp.float32)
        lru_h[...] = jnp.zeros(lru_h.shape, jnp.float32)
        state[...] = jnp.zeros(state.shape, jnp.float32)

    x = x_ref[0]
    u = x * (1.0 + sc_ref[0]) + sh_ref[0]
    proj = jnp.dot(_bf16(u), w_in_ref[...], preferred_element_type=jnp.float32)

    vec = vec_ref[...]

    def vrow(i):
        return vec[i:i + 1, :]

    cw = cw_ref[...]

    val = proj[:, OFF_A:OFF_A + D_GRP]
    gate = proj[:, OFF_A + D_GRP:OFF_A + 2 * D_GRP]
    hbuf[pl.ds(CONF_HIST, ts), :] = val * jax.nn.sigmoid(gate)
    conv = _causal_conv_long(hbuf, CONF_HIST, ts, cw[CW_A:CW_A + CONF_KERNEL], CONF_KERNEL)
    conv = conv + vrow(V_CONV_A_BIAS)
    hbuf[pl.ds(0, CONF_HIST), :] = hbuf[pl.ds(ts, CONF_HIST), :]
    ln_a = _layer_norm(conv, vrow(V_LN_A_G), vrow(V_LN_A_B))
    ymix[:, 0:D_GRP] = ln_a * jax.nn.sigmoid(ln_a)

    xbbuf[pl.ds(SMALL_HIST, ts), :] = proj[:, OFF_B:OFF_B + D_GRP]
    gb = proj[:, OFF_B + D_GRP:OFF_B + 2 * D_GRP]
    ub = _causal_conv(xbbuf, SMALL_HIST, ts, cw[CW_B:CW_B + LRU_CONV], LRU_CONV)
    ub = ub + vrow(V_CONV_B_BIAS)
    xbbuf[pl.ds(0, SMALL_HIST), :] = xbbuf[pl.ds(ts, SMALL_HIST), :]
    gates = jnp.dot(_bf16(ub), wgate_ref[...], preferred_element_type=jnp.float32)
    r_gate = jax.nn.sigmoid(gates[:, :D_GRP] + vrow(V_B_RG))
    i_gate = jax.nn.sigmoid(gates[:, D_GRP:] + vrow(V_B_IG))
    log_a = (-LRU_C) * r_gate * _softplus(-vrow(V_LRU_LAMBDA))
    a_sc = jnp.exp(log_a)
    b_sc = jnp.sqrt(-jnp.tanh(log_a) * (a_sc * a_sc + 1.0)) * (i_gate * ub)
    d = 1
    while d < ts:
        a_sh = _shift_rows(a_sc, d, 1.0)
        b_sh = _shift_rows(b_sc, d, 0.0)
        b_sc = a_sc * b_sh + b_sc
        a_sc = a_sc * a_sh
        d *= 2
    h = b_sc + a_sc * lru_h[0:1, :]
    lru_h[0:1, :] = h[ts - 1:ts, :]
    ymix[:, D_GRP:2 * D_GRP] = h * jax.nn.gelu(gb, approximate=True)

    gbd = proj[:, OFF_D:OFF_D + D_GRP]
    zbuf[pl.ds(SMALL_HIST, ts), :] = (proj[:, OFF_D + D_GRP:OFF_D + 2 * D_GRP]
                                      * proj[:, OFF_D + 2 * D_GRP:OFF_D + 3 * D_GRP])
    convd = _causal_conv(zbuf, SMALL_HIST, ts, cw[CW_D:CW_D + SHORT_CONV], SHORT_CONV)
    zbuf[pl.ds(0, SMALL_HIST), :] = zbuf[pl.ds(ts, SMALL_HIST), :]
    ymix[:, 3 * D_GRP:4 * D_GRP] = gbd * convd

    pc = proj[:, OFF_C:OFF_C + C_COLS]
    prev = jnp.where(_row_iota(pc.shape) == 0, pc_last[0:1, :], pltpu.roll(pc, 1, axis=0))
    pc_last[0:1, :] = pc[ts - 1:ts, :]
    xs = pc + (prev - pc) * mu_ref[...]
    r = xs[:, 0:D_GRP]
    k = xs[:, D_GRP:2 * D_GRP]
    v = xs[:, 2 * D_GRP:3 * D_GRP]
    z = xs[:, 3 * D_GRP:C_COLS]
    lane = lax.broadcasted_iota(jnp.int32, z.shape, 1)
    z_act = jnp.where(lane < LORA_W, jnp.tanh(z),
                      jnp.where(lane < LORA_W + LORA_A, z, jax.nn.sigmoid(z)))
    lora = jnp.dot(_bf16(z_act), wlora_ref[...], preferred_element_type=jnp.float32)
    wlog = -_softplus(-(vrow(V_W0) + lora[:, 0:D_GRP])) - 0.5
    lw = -jnp.exp(wlog)
    a_icl = jax.nn.sigmoid(vrow(V_A0) + lora[:, D_GRP:2 * D_GRP])
    g_out = lora[:, 2 * D_GRP:3 * D_GRP]

    kk = k * vrow(V_K_K)
    kk = kk * lax.rsqrt(_head_sum(kk * kk, bd_ref) + 1e-12)
    kf = k * (1.0 + (a_icl - 1.0) * vrow(V_K_A))

    y = _rwkv_chunked(kk, r, -(kk * a_icl), kf, v, lw, state)
    inv_n = 1.0 / HEAD_DIM
    mu_y = _head_sum(y, bd_ref) * inv_n
    yc = y - mu_y
    var_y = _head_sum(yc * yc, bd_ref) * inv_n
    y = yc * lax.rsqrt(var_y + RWKV_GN_EPS) * vrow(V_GN_G) + vrow(V_GN_B)
    bonus = _head_sum(r * kf * vrow(V_R_K), bd_ref) * v
    ymix[:, 2 * D_GRP:3 * D_GRP] = (y + bonus) * g_out

    mix = jnp.dot(_bf16(ymix[...]), w_out_ref[...], preferred_element_type=jnp.float32)
    alpha_dn = ln_ref[2:3, :]
    res = alpha_dn * x + (1.0 + g_ref[0]) * mix
    o_ref[0] = _layer_norm(res, ln_ref[0:1, :], ln_ref[1:2, :])


def _token_mixer_layer(x, sh, sc, g, w_in, w_out, cw, vec, mu, ln, wgate, wlora, bd, ts):
    bsz, s, d = x.shape
    grid = (bsz, s // ts)
    full2 = lambda b, i: (0, 0)
    mod_spec = pl.BlockSpec((1, 1, d), lambda b, i: (b, 0, 0))
    f32 = jnp.float32

    def tile(n_cols):
        return pltpu.VMEM((ts, n_cols), f32)

    return pl.pallas_call(
        _mixer_kernel,
        grid=grid,
        in_specs=[
            pl.BlockSpec((1, ts, d), lambda b, i: (b, i, 0)),
            mod_spec, mod_spec, mod_spec,
            pl.BlockSpec(w_in.shape, full2),
            pl.BlockSpec(w_out.shape, full2),
            pl.BlockSpec(cw.shape, full2),
            pl.BlockSpec(vec.shape, full2),
            pl.BlockSpec(mu.shape, full2),
            pl.BlockSpec(ln.shape, full2),
            pl.BlockSpec(wgate.shape, full2),
            pl.BlockSpec(wlora.shape, full2),
            pl.BlockSpec(bd.shape, full2),
        ],
        out_specs=pl.BlockSpec((1, ts, d), lambda b, i: (b, i, 0)),
        out_shape=jax.ShapeDtypeStruct((bsz, s, d), f32),
        scratch_shapes=[
            pltpu.VMEM((CONF_HIST + ts, D_GRP), f32),
            pltpu.VMEM((SMALL_HIST + ts, D_GRP), f32),
            pltpu.VMEM((SMALL_HIST + ts, D_GRP), f32),
            pltpu.VMEM((SUBLANES, C_COLS), f32),
            pltpu.VMEM((SUBLANES, D_GRP), f32),
            pltpu.VMEM((N_HEADS, HEAD_DIM, HEAD_DIM), f32),
            tile(D_MODEL),
        ],
        compiler_params=pltpu.CompilerParams(
            dimension_semantics=("arbitrary", "arbitrary"),
            vmem_limit_bytes=VMEM_LIMIT_BYTES),
        name="token_mixers",
    )(x, sh, sc, g, w_in, w_out, cw, vec, mu, ln, wgate, wlora, bd)


def _first_index(mask, idx, sentinel):
    return jnp.min(jnp.where(mask, idx, sentinel), axis=0, keepdims=True)


def _router_kernel(x_ref, sh_ref, sc_ref, wr_ref, eb_ref, su_ref, sl_ref,
                   slot_ref, wt_ref, plan_ref):
    x = x_ref[0]
    u = x * (1.0 + sc_ref[0]) + sh_ref[0]
    tm = u.shape[0]
    logits = _dot_f32(wr_ref[...], u, ((1,), (1,)))
    scores = jax.nn.sigmoid(logits)
    biased = scores + eb_ref[...]
    neg_inf = jnp.float32(-jnp.inf)
    sub = lax.broadcasted_iota(jnp.int32, (E_PER_GROUP, tm), 0).astype(jnp.float32)

    groups = [biased[g * E_PER_GROUP:(g + 1) * E_PER_GROUP, :] for g in range(N_GROUPS)]
    gscore = []
    for blk in groups:
        m1 = jnp.max(blk, axis=0, keepdims=True)
        first = _first_index(blk == m1, sub, float(E_PER_GROUP))
        m2 = jnp.max(jnp.where(sub == first, neg_inf, blk), axis=0, keepdims=True)
        gscore.append(m1 + m2)
    gs = jnp.concatenate(gscore, axis=0)
    gidx = lax.broadcasted_iota(jnp.int32, (N_GROUPS, tm), 0).astype(jnp.float32)
    keep = jnp.zeros((N_GROUPS, tm), jnp.float32)
    for _ in range(TOPK_GROUPS):
        m = jnp.max(gs, axis=0, keepdims=True)
        first = _first_index(gs == m, gidx, float(N_GROUPS))
        sel = gidx == first
        keep = jnp.where(sel, 1.0, keep)
        gs = jnp.where(sel, neg_inf, gs)

    masked = [jnp.where(keep[g:g + 1, :] > 0.5, groups[g], neg_inf) for g in range(N_GROUPS)]
    eidx = [sub + float(g * E_PER_GROUP) for g in range(N_GROUPS)]
    chosen = [jnp.zeros((E_PER_GROUP, tm), jnp.float32) for _ in range(N_GROUPS)]
    for _ in range(TOP_K):
        m = functools.reduce(jnp.maximum, [jnp.max(b, axis=0, keepdims=True) for b in masked])
        first = functools.reduce(
            jnp.minimum,
            [_first_index(masked[g] == m, eidx[g], float(N_EXPERTS)) for g in range(N_GROUPS)])
        for g in range(N_GROUPS):
            sel = eidx[g] == first
            chosen[g] = jnp.where(sel, 1.0, chosen[g])
            masked[g] = jnp.where(sel, neg_inf, masked[g])

    picked = [jnp.where(chosen[g] > 0.5, scores[g * E_PER_GROUP:(g + 1) * E_PER_GROUP, :], 0.0)
              for g in range(N_GROUPS)]
    total = functools.reduce(jnp.add, [jnp.sum(p, axis=0, keepdims=True) for p in picked])
    wts = jnp.concatenate([p / total * ROUTED_SCALE for p in picked], axis=0)

    sel = jnp.concatenate(chosen, axis=0)
    sel_bf = _bf16(sel)
    rank = jnp.dot(sel_bf, su_ref[...], preferred_element_type=jnp.float32)
    cnt = jnp.sum(sel, axis=1, keepdims=True)
    off = _dot_f32(sl_ref[...].astype(jnp.float32), jnp.broadcast_to(cnt, (N_EXPERTS, LANES)))
    slot = off[:, 0:1] + rank
    kidx = jnp.dot(sl_ref[...], sel_bf, preferred_element_type=jnp.float32)
    slot_rows, wt_rows = [], []
    for kk in range(TOP_K):
        pick = (sel > 0.5) & (kidx == float(kk))
        slot_rows.append(jnp.sum(jnp.where(pick, slot, 0.0), axis=0, keepdims=True))
        wt_rows.append(jnp.sum(jnp.where(pick, wts, 0.0), axis=0, keepdims=True))
    lane_pad = [jnp.zeros((LANES - TOP_K, tm), jnp.float32)]
    slot_ref[0] = (jnp.concatenate(slot_rows + lane_pad, axis=0).T
                   * float(SLAB_ROWS)).astype(jnp.int32)
    wt_ref[0] = jnp.concatenate(wt_rows + lane_pad, axis=0).T
    e_sub = lax.broadcasted_iota(jnp.int32, (N_EXPERTS, LANES), 0)
    e_lane = lax.broadcasted_iota(jnp.int32, (N_EXPERTS, LANES), 1)
    diag = e_sub == e_lane
    off_row = jnp.sum(jnp.where(diag, off, 0.0), axis=0, keepdims=True)
    cnt_row = jnp.sum(jnp.where(diag, jnp.broadcast_to(cnt, (N_EXPERTS, LANES)), 0.0),
                      axis=0, keepdims=True)
    plan_ref[0] = jnp.concatenate([off_row, cnt_row], axis=0).astype(jnp.int32)


def _router_layer(x, sh, sc, w_router_t, e_bias_col, su, sl, tm):
    bsz, s, d = x.shape
    per_b = s // tm
    n_tiles = bsz * per_b
    mod_spec = pl.BlockSpec((1, 1, d), lambda b, i: (b, 0, 0))
    const2 = lambda b, i: (0, 0)
    tile3 = lambda b, i: (b * per_b + i, 0, 0)
    return pl.pallas_call(
        _router_kernel,
        grid=(bsz, per_b),
        in_specs=[
            pl.BlockSpec((1, tm, d), lambda b, i: (b, i, 0)),
            mod_spec, mod_spec,
            pl.BlockSpec(w_router_t.shape, const2),
            pl.BlockSpec(e_bias_col.shape, const2),
            pl.BlockSpec(su.shape, const2),
            pl.BlockSpec(sl.shape, const2),
        ],
        out_specs=[
            pl.BlockSpec((1, tm, LANES), tile3),
            pl.BlockSpec((1, tm, LANES), tile3),
            pl.BlockSpec((1, 2, LANES), tile3),
        ],
        out_shape=[
            jax.ShapeDtypeStruct((n_tiles, tm, LANES), jnp.int32),
            jax.ShapeDtypeStruct((n_tiles, tm, LANES), jnp.float32),
            jax.ShapeDtypeStruct((n_tiles, 2, LANES), jnp.int32),
        ],
        compiler_params=pltpu.CompilerParams(
            dimension_semantics=("arbitrary", "arbitrary"),
            vmem_limit_bytes=VMEM_LIMIT_BYTES),
        name="moe_router",
    )(x, sh, sc, w_router_t, e_bias_col, su, sl)


def _swiglu(u_bf16, w13, w2):
    hcat = jnp.dot(u_bf16, w13, preferred_element_type=jnp.float32)
    half = hcat.shape[1] // 2
    gate, up = hcat[:, :half], hcat[:, half:]
    act = gate * jax.nn.sigmoid(gate) * up
    return jnp.dot(_bf16(act), w2, preferred_element_type=jnp.float32)


N_DMA_QUEUES = 2
W_COPIES = 2 * N_DMA_QUEUES


def _weight_copies(layer, w13_hbm, w2_hbm, w13_buf, w2_buf, sems, e, slot):
    copies = []
    for hbm, buf in ((w13_hbm, w13_buf), (w2_hbm, w2_buf)):
        rows = hbm.shape[2] // N_DMA_QUEUES
        for part in range(N_DMA_QUEUES):
            sl = pl.ds(part * rows, rows)
            copies.append(pltpu.make_async_copy(hbm.at[layer, e, sl], buf.at[slot, sl],
                                                sems.at[len(copies), slot]))
    return copies


def _start_weight_copies(layer, w13_hbm, w2_hbm, w13_buf, w2_buf, sems, e, slot):
    for n, cp in enumerate(_weight_copies(layer, w13_hbm, w2_hbm, w13_buf, w2_buf, sems, e, slot)):
        cp.start(priority=n % N_DMA_QUEUES)


def _expert_kernel(layer, plan_ref, slot_ref, wt_ref, x_ref, sh_ref, sc_ref, g_ref, w13_hbm,
                   w2_hbm, ws13_ref, ws2_ref, ln_ref, o_ref, rows_f, xf, w13_buf, w2_buf, sems):
    i = pl.program_id(0)
    tm = x_ref.shape[0]
    n_exp = w13_hbm.shape[1]
    n_col = x_ref.shape[1] // LANES

    @pl.when(i == 0)
    def _clear():
        xf[...] = jnp.zeros(xf.shape, xf.dtype)

    for e0 in range(W_AHEAD):
        _start_weight_copies(layer, w13_hbm, w2_hbm, w13_buf, w2_buf, sems, e0, e0)

    u = x_ref[...] * (1.0 + sc_ref[0]) + sh_ref[0]
    for j in range(n_col):
        rows_f[pl.ds(j, tm, stride=SLAB_ROWS), :] = u[:, j * LANES:(j + 1) * LANES]

    def scatter(it, carry):
        for h in range(TOKEN_UNROLL):
            t = it * TOKEN_UNROLL + h
            row = rows_f[pl.ds(pl.multiple_of(t * SLAB_ROWS, SLAB_ROWS), SLAB_ROWS), :]
            for k in range(TOP_K):
                dst = pl.multiple_of(slot_ref[t * TOP_K + k], SLAB_ROWS)
                xf[pl.ds(dst, SLAB_ROWS), :] = row
        return carry

    lax.fori_loop(0, tm // TOKEN_UNROLL, scatter, 0)

    def expert(e, carry):
        slot = lax.rem(e, W_SLOTS)
        for cp in _weight_copies(layer, w13_hbm, w2_hbm, w13_buf, w2_buf, sems, e, slot):
            cp.wait()

        @pl.when(e + W_AHEAD < n_exp)
        def _prefetch():
            nxt = e + W_AHEAD
            _start_weight_copies(layer, w13_hbm, w2_hbm, w13_buf, w2_buf, sems, nxt,
                                 lax.rem(nxt, W_SLOTS))

        first = plan_ref[i * 2 * LANES + e]
        count = plan_ref[i * 2 * LANES + LANES + e]
        n_blocks = lax.div(count + (EXPERT_ROWS - 1), EXPERT_ROWS)

        def block(j, c2):
            base = (first + j * EXPERT_ROWS) * SLAB_ROWS
            old = [xf[pl.ds(base + q, EXPERT_ROWS, stride=SLAB_ROWS), :] for q in range(n_col)]
            xb = jnp.concatenate([_bf16(o) for o in old], axis=1)
            y = _swiglu(xb, w13_buf[slot], w2_buf[slot])
            live = (lax.broadcasted_iota(jnp.int32, (EXPERT_ROWS, LANES), 0)
                    < count - j * EXPERT_ROWS)
            for q in range(n_col):
                xf[pl.ds(base + q, EXPERT_ROWS, stride=SLAB_ROWS), :] = jnp.where(
                    live, y[:, q * LANES:(q + 1) * LANES], old[q])
            return c2

        lax.fori_loop(0, n_blocks, block, 0)
        return carry

    lax.fori_loop(0, n_exp, expert, 0)

    def gather(it, carry):
        for h in range(TOKEN_UNROLL):
            t = it * TOKEN_UNROLL + h
            acc = jnp.zeros((SLAB_ROWS, LANES), jnp.float32)
            for k in range(TOP_K):
                src = pl.multiple_of(slot_ref[t * TOP_K + k], SLAB_ROWS)
                acc = acc + wt_ref[t * TOP_K + k] * xf[pl.ds(src, SLAB_ROWS), :]
            rows_f[pl.ds(pl.multiple_of(t * SLAB_ROWS, SLAB_ROWS), SLAB_ROWS), :] = acc
        return carry

    lax.fori_loop(0, tm // TOKEN_UNROLL, gather, 0)
    routed = jnp.concatenate([rows_f[pl.ds(j, tm, stride=SLAB_ROWS), :] for j in range(n_col)],
                             axis=1)
    x = x_ref[...]
    u = x * (1.0 + sc_ref[0]) + sh_ref[0]
    ffn = routed + _swiglu(_bf16(u), ws13_ref[...], ws2_ref[...])
    res = ln_ref[2:3, :] * x + (1.0 + g_ref[0]) * ffn
    o_ref[...] = _layer_norm(res, ln_ref[0:1, :], ln_ref[1:2, :])


def _expert_layer(x, sh, sc, g, plan, slots, wts, layer, w13, w2, ws13, ws2, ln, tm):
    bsz, s, d = x.shape
    per_b = s // tm
    n_tiles = bsz * per_b
    mod_spec = pl.BlockSpec((1, 1, d), lambda i, plan: (i // per_b, 0, 0))
    full2 = lambda i, plan: (0, 0)
    once = pl.Buffered(1)
    tile_smem = pl.BlockSpec((TOP_K * tm,), lambda i, plan: (i,), memory_space=pltpu.SMEM)
    slab_rows = (TOP_K * tm + EXPERT_ROWS) * SLAB_ROWS
    grid_spec = pltpu.PrefetchScalarGridSpec(
        num_scalar_prefetch=1,
        grid=(n_tiles,),
        in_specs=[
            tile_smem, tile_smem,
            pl.BlockSpec((tm, d), lambda i, plan: (i, 0), pipeline_mode=once),
            mod_spec, mod_spec, mod_spec,
            pl.BlockSpec(memory_space=pl.ANY),
            pl.BlockSpec(memory_space=pl.ANY),
            pl.BlockSpec(ws13.shape, full2, pipeline_mode=once),
            pl.BlockSpec(ws2.shape, full2, pipeline_mode=once),
            pl.BlockSpec(ln.shape, full2),
        ],
        out_specs=pl.BlockSpec((tm, d), lambda i, plan: (i, 0), pipeline_mode=once),
        scratch_shapes=[
            pltpu.VMEM((tm * SLAB_ROWS, LANES), jnp.float32),
            pltpu.VMEM((slab_rows, LANES), jnp.float32),
            pltpu.VMEM((W_SLOTS,) + w13.shape[2:], w13.dtype),
            pltpu.VMEM((W_SLOTS,) + w2.shape[2:], w2.dtype),
            pltpu.SemaphoreType.DMA((W_COPIES, W_SLOTS)),
        ],
    )
    out = pl.pallas_call(
        functools.partial(_expert_kernel, layer),
        grid_spec=grid_spec,
        out_shape=jax.ShapeDtypeStruct((bsz * s, d), jnp.float32),
        compiler_params=pltpu.CompilerParams(
            dimension_semantics=("arbitrary",),
            vmem_limit_bytes=VMEM_LIMIT_BYTES),
        name="moe_experts",
    )(plan.reshape(-1), slots[:, :, :TOP_K].reshape(-1), wts[:, :, :TOP_K].reshape(-1),
      x.reshape(bsz * s, d), sh, sc, g,
      w13, w2, ws13, ws2, ln)
    return out.reshape(bsz, s, d)


def _block_diag(w):
    h, n, _ = w.shape
    eye = jnp.eye(h, dtype=w.dtype)
    return (eye[:, None, :, None] * w[:, :, None, :]).reshape(h * n, h * n)


def _pad_rows(a, n_rows):
    return jnp.concatenate([a, jnp.zeros((n_rows - a.shape[0], a.shape[1]), a.dtype)], axis=0)


def _tile_rows(seq_len, want):
    t = min(want, seq_len)
    assert seq_len % t == 0 and t % RWKV_CHUNK == 0
    return t


def kernel(x, c, w_mod, b_mod, w_in, w_out, conv_a, conv_a_bias, ln_a_g, ln_a_b, conv_b, conv_b_bias, w_rg, b_rg, w_ig, b_ig, lru_lambda, mu_c, w0, w_w2, a0, w_a2, w_g2, k_k, k_a, r_k, gn_g, gn_b, conv_d, ln1_g, ln1_b, w_router, e_bias, w13, w2, ws13, ws2, ln2_g, ln2_b):
    n_layers = w_mod.shape[0]
    bsz, s, d = x.shape
    alpha_dn = (2.0 * n_layers) ** 0.25
    ts_mix = _tile_rows(s, 512)
    tm_moe = _tile_rows(s, 1024)
    bf = jnp.bfloat16

    mod = _modulation(c, w_mod, b_mod)
    bd = _block_diag(jnp.ones((N_HEADS, HEAD_DIM, HEAD_DIM), bf))
    alpha_row = jnp.full((1, d), alpha_dn, jnp.float32)
    tri_tok = jnp.triu(jnp.ones((tm_moe, tm_moe), bf), k=1)
    tri_exp = jnp.tril(jnp.ones((N_EXPERTS, N_EXPERTS), bf), k=-1)
    w13_bf, w2_bf = w13.astype(bf), w2.astype(bf)

    for l in range(n_layers):
        sh1, sc1, g1, sh2, sc2, g2 = [mod[l, :, i * d:(i + 1) * d].reshape(bsz, 1, d)
                                      for i in range(6)]
        cw = _pad_rows(jnp.concatenate([conv_a[l], conv_b[l], conv_d[l]], axis=0), N_CW_ROWS)
        vec = _pad_rows(jnp.stack([
            conv_a_bias[l], ln_a_g[l], ln_a_b[l], conv_b_bias[l], b_rg[l], b_ig[l], lru_lambda[l],
            w0[l], a0[l], k_k[l], k_a[l], r_k[l].reshape(D_GRP), gn_g[l], gn_b[l]], axis=0),
            N_VEC_ROWS)
        wgate = jnp.concatenate([_block_diag(w_rg[l]), _block_diag(w_ig[l])], axis=1).astype(bf)
        wlora = jnp.zeros((LORA_W + LORA_A + LORA_G, 3 * D_GRP), jnp.float32)
        wlora = wlora.at[0:LORA_W, 0:D_GRP].set(w_w2[l])
        wlora = wlora.at[LORA_W:LORA_W + LORA_A, D_GRP:2 * D_GRP].set(w_a2[l])
        wlora = wlora.at[LORA_W + LORA_A:, 2 * D_GRP:].set(w_g2[l]).astype(bf)
        ln1 = jnp.concatenate([ln1_g[l][None], ln1_b[l][None], alpha_row], axis=0)
        ln2 = jnp.concatenate([ln2_g[l][None], ln2_b[l][None], alpha_row], axis=0)

        x = _token_mixer_layer(x, sh1, sc1, g1, w_in[l].astype(bf), w_out[l].astype(bf), cw, vec,
                               mu_c[l][None], ln1, wgate, wlora, bd, ts_mix)
        slots, wts, plan = _router_layer(x, sh2, sc2, w_router[l].T, e_bias[l][:, None],
                                         tri_tok, tri_exp, tm_moe)
        x = _expert_layer(x, sh2, sc2, g2, plan, slots, wts, l, w13_bf, w2_bf,
                          ws13[l].astype(bf), ws2[l].astype(bf), ln2, tm_moe)
    return x
```

```python
import functools

import jax
import jax.numpy as jnp
from jax import lax
from jax.experimental import pallas as pl
from jax.experimental.pallas import tpu as pltpu

D_MODEL = 1024
HEAD_DIM = 64
D_GRP = 256
N_HEADS = D_GRP // HEAD_DIM
CONF_KERNEL = 31
LRU_CONV = 4
LRU_C = 8.0
SHORT_CONV = 3
LORA_W, LORA_A, LORA_G = 32, 32, 64
C_COLS = 3 * D_GRP + LORA_W + LORA_A + LORA_G
P_IN = 4 * D_GRP + C_COLS + 3 * D_GRP
RWKV_GN_EPS = 64e-5
N_EXPERTS = 64
TOP_K = 8
N_GROUPS = 8
TOPK_GROUPS = 4
E_PER_GROUP = N_EXPERTS // N_GROUPS
ROUTED_SCALE = 2.5
LN_EPS = 1e-5

OFF_A = 0
OFF_B = 2 * D_GRP
OFF_C = 4 * D_GRP
OFF_D = OFF_C + C_COLS

SUBLANES = 8
LANES = 128
VMEM_LIMIT_BYTES = 58 * 1024 * 1024

SLAB_ROWS = D_MODEL // LANES
EXPERT_ROWS = 144
TOKEN_UNROLL = 4
W_SLOTS = 4
W_AHEAD = W_SLOTS - 1
RWKV_CHUNK = 64
CONF_HIST = 32
SMALL_HIST = 8

(V_CONV_A_BIAS, V_LN_A_G, V_LN_A_B, V_CONV_B_BIAS, V_B_RG, V_B_IG, V_LRU_LAMBDA, V_W0, V_A0,
 V_K_K, V_K_A, V_R_K, V_GN_G, V_GN_B) = range(14)
N_VEC_ROWS = 16
CW_A, CW_B, CW_D = 0, CONF_KERNEL, CONF_KERNEL + LRU_CONV
N_CW_ROWS = 40


def _bf16(x):
    return x.astype(jnp.bfloat16)


def _split(a):
    hi = _bf16(a)
    lo = _bf16(a - hi.astype(jnp.float32))
    return hi, lo


def _dot_f32(a, b, dims=((1,), (0,))):
    a_hi, a_lo = _split(a)
    b_hi, b_lo = _split(b)
    dn = (dims, ((), ()))
    f = functools.partial(lax.dot_general, dimension_numbers=dn,
                          preferred_element_type=jnp.float32)
    return f(a_hi, b_hi) + (f(a_hi, b_lo) + f(a_lo, b_hi))


def _dot_lhs_f32(a, b_bf16):
    a_hi, a_lo = _split(a)
    f = functools.partial(jnp.dot, preferred_element_type=jnp.float32)
    return f(a_hi, b_bf16) + f(a_lo, b_bf16)


def _layer_norm(x, g, b):
    mu = jnp.mean(x, axis=-1, keepdims=True)
    xc = x - mu
    var = jnp.mean(xc * xc, axis=-1, keepdims=True)
    return xc * lax.rsqrt(var + LN_EPS) * g + b


def _softplus(x):
    return jnp.maximum(x, 0.0) + jnp.log1p(jnp.exp(-jnp.abs(x)))


def _row_iota(shape):
    return lax.broadcasted_iota(jnp.int32, shape, 0)


def _shift_rows(x, d, fill):
    rolled = pltpu.roll(x, d, axis=0)
    return jnp.where(_row_iota(x.shape) >= d, rolled, fill)


def _mod_kernel(c_ref, w_ref, b_ref, o_ref):
    c = c_ref[...]
    c_act = c * jax.nn.sigmoid(c)
    o_ref[0] = _dot_f32(c_act, w_ref[0]) + b_ref[0]


def _modulation(c, w_mod, b_mod):
    n_layers, d, d6 = w_mod.shape
    bsz = c.shape[0]
    tn = D_MODEL
    return pl.pallas_call(
        _mod_kernel,
        grid=(n_layers, d6 // tn),
        in_specs=[
            pl.BlockSpec((bsz, d), lambda l, j: (0, 0)),
            pl.BlockSpec((1, d, tn), lambda l, j: (l, 0, j)),
            pl.BlockSpec((1, 1, tn), lambda l, j: (l, 0, j)),
        ],
        out_specs=pl.BlockSpec((1, bsz, tn), lambda l, j: (l, 0, j)),
        out_shape=jax.ShapeDtypeStruct((n_layers, bsz, d6), jnp.float32),
        compiler_params=pltpu.CompilerParams(
            dimension_semantics=("arbitrary", "arbitrary"),
            vmem_limit_bytes=VMEM_LIMIT_BYTES),
        name="adaln_modulation",
    )(c, w_mod, b_mod.reshape(n_layers, 1, d6))


def _causal_conv(buf_ref, hist, ts, w_rows, n_taps):
    acc = None
    for k in range(n_taps):
        off = hist - (n_taps - 1) + k
        term = w_rows[k:k + 1, :] * buf_ref[pl.ds(off, ts), :]
        acc = term if acc is None else acc + term
    return acc


def _causal_conv_long(buf_ref, hist, ts, w_rows, n_taps):
    ext = ts + SUBLANES
    out = None
    for r in range(SUBLANES):
        part = None
        for q in range((n_taps - 1 - r) // SUBLANES + 1):
            lag = SUBLANES * q + r
            term = (w_rows[n_taps - 1 - lag:n_taps - lag, :]
                    * buf_ref[pl.ds(hist - SUBLANES - SUBLANES * q, ext), :])
            part = term if part is None else part + term
        shifted = part if r == 0 else pltpu.roll(part, r, axis=0)
        piece = shifted[SUBLANES:, :]
        out = piece if out is None else out + piece
    return out


def _head_sum(x, bd_ref):
    return _dot_lhs_f32(x, bd_ref[...])


_NN = ((2,), (1,))
_NT = ((2,), (2,))
_TN = ((1,), (1,))


def _bdot(a, b, dims):
    return lax.dot_general(_bf16(a), _bf16(b), (dims, ((0,), (0,))),
                           preferred_element_type=jnp.float32)


def _to_problems(x):
    n_chunks = x.shape[0] // RWKV_CHUNK
    return jnp.stack([x[c * RWKV_CHUNK:(c + 1) * RWKV_CHUNK, h * HEAD_DIM:(h + 1) * HEAD_DIM]
                      for c in range(n_chunks) for h in range(N_HEADS)], axis=0)


def _rwkv_chunked(kk, r, alpha, kf, v, lw, state):
    ts = kk.shape[0]
    n_chunks = ts // RWKV_CHUNK
    cl = lw
    seg_row = _row_iota(cl.shape) % RWKV_CHUNK
    d = 1
    while d < RWKV_CHUNK:
        cl = cl + jnp.where(seg_row >= d, pltpu.roll(cl, d, axis=0), 0.0)
        d *= 2
    cl_end = cl.reshape(n_chunks, RWKV_CHUNK, D_GRP)[:, RWKV_CHUNK - 1:RWKV_CHUNK, :]
    cl_end_rows = jnp.broadcast_to(cl_end, (n_chunks, RWKV_CHUNK, D_GRP)).reshape(ts, D_GRP)
    g_inv = jnp.exp(-cl)
    tail = jnp.exp(cl_end_rows - cl)
    g_end = jnp.exp(cl_end)
    bt = _to_problems(_bf16(kk * jnp.exp(cl - lw)))
    rt = _to_problems(_bf16(r * jnp.exp(cl)))
    at = _to_problems(_bf16(alpha * g_inv))
    kt = _to_problems(_bf16(kf * g_inv))
    ab = _to_problems(_bf16(alpha * tail))
    kb = _to_problems(_bf16(kf * tail))
    vp = _to_problems(_bf16(v))

    c_sz = RWKV_CHUNK
    tri_r = lax.broadcasted_iota(jnp.int32, (1, c_sz, c_sz), 1)
    tri_c = lax.broadcasted_iota(jnp.int32, (1, c_sz, c_sz), 2)
    strict = tri_r > tri_c
    incl = tri_r >= tri_c
    eye = (tri_r == tri_c).astype(jnp.float32)

    amat = _bdot(jnp.concatenate([bt, rt], axis=1), jnp.concatenate([at, kt], axis=1), _NT)
    a_ba = jnp.where(strict, amat[:, :c_sz, :c_sz], 0.0)
    a_bk = jnp.where(strict, amat[:, :c_sz, c_sz:], 0.0)
    a_ra = jnp.where(incl, amat[:, c_sz:, :c_sz], 0.0)
    a_rk = jnp.where(incl, amat[:, c_sz:, c_sz:], 0.0)
    tinv = eye + a_ba
    pw = a_ba
    step = 2
    while step < c_sz:
        pw = _bdot(pw, pw, _NN)
        tinv = tinv + _bdot(pw, tinv, _NN)
        step *= 2
    tx = _bdot(tinv, jnp.concatenate([bt, _bf16(_bdot(a_bk, vp, _NN))], axis=2), _NN)
    w_t, u_t = tx[:, :, :HEAD_DIM], tx[:, :, HEAD_DIM:]
    y_k = _bdot(a_rk, vp, _NN)

    s_cur = state[...]
    y_rows = []
    for c in range(n_chunks):
        p = slice(c * N_HEADS, (c + 1) * N_HEADS)
        res = _bdot(jnp.concatenate([_bf16(w_t[p]), rt[p]], axis=1), s_cur, _NT)
        u_c = res[:, :c_sz, :] + u_t[p]
        y_c = res[:, c_sz:, :] + _bdot(a_ra[p], u_c, _NN) + y_k[p]
        g_c = jnp.stack([g_end[c, :, h * HEAD_DIM:(h + 1) * HEAD_DIM] for h in range(N_HEADS)],
                        axis=0)
        s_cur = s_cur * g_c + _bdot(jnp.concatenate([_bf16(u_c), vp[p]], axis=1),
                                    jnp.concatenate([ab[p], kb[p]], axis=1), _TN)
        y_rows.append(jnp.concatenate([y_c[h] for h in range(N_HEADS)], axis=1))
    state[...] = s_cur
    return jnp.concatenate(y_rows, axis=0)


def _mixer_kernel(x_ref, sh_ref, sc_ref, g_ref, w_in_ref, w_out_ref, cw_ref, vec_ref, mu_ref,
                  ln_ref, wgate_ref, wlora_ref, bd_ref, o_ref,
                  hbuf, xbbuf, zbuf, pc_last, lru_h, state, ymix):
    ts = x_ref.shape[1]
    s_idx = pl.program_id(1)

    @pl.when(s_idx == 0)
    def _reset():
        hbuf[pl.ds(0, CONF_HIST), :] = jnp.zeros((CONF_HIST, D_GRP), jnp.float32)
        xbbuf[pl.ds(0, SMALL_HIST), :] = jnp.zeros((SMALL_HIST, D_GRP), jnp.float32)
        zbuf[pl.ds(0, SMALL_HIST), :] = jnp.zeros((SMALL_HIST, D_GRP), jnp.float32)
        pc_last[...] = jnp.zeros(pc_last.shape, jnp.float32)
        lru_h[...] = jnp.zeros(lru_h.shape, jnp.float32)
        state[...] = jnp.zeros(state.shape, jnp.float32)

    x = x_ref[0]
    u = x * (1.0 + sc_ref[0]) + sh_ref[0]
    proj = jnp.dot(_bf16(u), w_in_ref[...], preferred_element_type=jnp.float32)

    vec = vec_ref[...]

    def vrow(i):
        return vec[i:i + 1, :]

    cw = cw_ref[...]

    val = proj[:, OFF_A:OFF_A + D_GRP]
    gate = proj[:, OFF_A + D_GRP:OFF_A + 2 * D_GRP]
    hbuf[pl.ds(CONF_HIST, ts), :] = val * jax.nn.sigmoid(gate)
    conv = _causal_conv_long(hbuf, CONF_HIST, ts, cw[CW_A:CW_A + CONF_KERNEL], CONF_KERNEL)
    conv = conv + vrow(V_CONV_A_BIAS)
    hbuf[pl.ds(0, CONF_HIST), :] = hbuf[pl.ds(ts, CONF_HIST), :]
    ln_a = _layer_norm(conv, vrow(V_LN_A_G), vrow(V_LN_A_B))
    ymix[:, 0:D_GRP] = ln_a * jax.nn.sigmoid(ln_a)

    xbbuf[pl.ds(SMALL_HIST, ts), :] = proj[:, OFF_B:OFF_B + D_GRP]
    gb = proj[:, OFF_B + D_GRP:OFF_B + 2 * D_GRP]
    ub = _causal_conv(xbbuf, SMALL_HIST, ts, cw[CW_B:CW_B + LRU_CONV], LRU_CONV)
    ub = ub + vrow(V_CONV_B_BIAS)
    xbbuf[pl.ds(0, SMALL_HIST), :] = xbbuf[pl.ds(ts, SMALL_HIST), :]
    gates = jnp.dot(_bf16(ub), wgate_ref[...], preferred_element_type=jnp.float32)
    r_gate = jax.nn.sigmoid(gates[:, :D_GRP] + vrow(V_B_RG))
    i_gate = jax.nn.sigmoid(gates[:, D_GRP:] + vrow(V_B_IG))
    log_a = (-LRU_C) * r_gate * _softplus(-vrow(V_LRU_LAMBDA))
    a_sc = jnp.exp(log_a)
    b_sc = jnp.sqrt(-jnp.tanh(log_a) * (a_sc * a_sc + 1.0)) * (i_gate * ub)
    d = 1
    while d < ts:
        a_sh = _shift_rows(a_sc, d, 1.0)
        b_sh = _shift_rows(b_sc, d, 0.0)
        b_sc = a_sc * b_sh + b_sc
        a_sc = a_sc * a_sh
        d *= 2
    h = b_sc + a_sc * lru_h[0:1, :]
    lru_h[0:1, :] = h[ts - 1:ts, :]
    ymix[:, D_GRP:2 * D_GRP] = h * jax.nn.gelu(gb, approximate=True)

    gbd = proj[:, OFF_D:OFF_D + D_GRP]
    zbuf[pl.ds(SMALL_HIST, ts), :] = (proj[:, OFF_D + D_GRP:OFF_D + 2 * D_GRP]
                                      * proj[:, OFF_D + 2 * D_GRP:OFF_D + 3 * D_GRP])
    convd = _causal_conv(zbuf, SMALL_HIST, ts, cw[CW_D:CW_D + SHORT_CONV], SHORT_CONV)
    zbuf[pl.ds(0, SMALL_HIST), :] = zbuf[pl.ds(ts, SMALL_HIST), :]
    ymix[:, 3 * D_GRP:4 * D_GRP] = gbd * convd

    pc = proj[:, OFF_C:OFF_C + C_COLS]
    prev = jnp.where(_row_iota(pc.shape) == 0, pc_last[0:1, :], pltpu.roll(pc, 1, axis=0))
    pc_last[0:1, :] = pc[ts - 1:ts, :]
    xs = pc + (prev - pc) * mu_ref[...]
    r = xs[:, 0:D_GRP]
    k = xs[:, D_GRP:2 * D_GRP]
    v = xs[:, 2 * D_GRP:3 * D_GRP]
    z = xs[:, 3 * D_GRP:C_COLS]
    lane = lax.broadcasted_iota(jnp.int32, z.shape, 1)
    z_act = jnp.where(lane < LORA_W, jnp.tanh(z),
                      jnp.where(lane < LORA_W + LORA_A, z, jax.nn.sigmoid(z)))
    lora = jnp.dot(_bf16(z_act), wlora_ref[...], preferred_element_type=jnp.float32)
    wlog = -_softplus(-(vrow(V_W0) + lora[:, 0:D_GRP])) - 0.5
    lw = -jnp.exp(wlog)
    a_icl = jax.nn.sigmoid(vrow(V_A0) + lora[:, D_GRP:2 * D_GRP])
    g_out = lora[:, 2 * D_GRP:3 * D_GRP]

    kk = k * vrow(V_K_K)
    kk = kk * lax.rsqrt(_head_sum(kk * kk, bd_ref) + 1e-12)
    kf = k * (1.0 + (a_icl - 1.0) * vrow(V_K_A))

    y = _rwkv_chunked(kk, r, -(kk * a_icl), kf, v, lw, state)
    inv_n = 1.0 / HEAD_DIM
    mu_y = _head_sum(y, bd_ref) * inv_n
    yc = y - mu_y
    var_y = _head_sum(yc * yc, bd_ref) * inv_n
    y = yc * lax.rsqrt(var_y + RWKV_GN_EPS) * vrow(V_GN_G) + vrow(V_GN_B)
    bonus = _head_sum(r * kf * vrow(V_R_K), bd_ref) * v
    ymix[:, 2 * D_GRP:3 * D_GRP] = (y + bonus) * g_out

    mix = jnp.dot(_bf16(ymix[...]), w_out_ref[...], preferred_element_type=jnp.float32)
    alpha_dn = ln_ref[2:3, :]
    res = alpha_dn * x + (1.0 + g_ref[0]) * mix
    o_ref[0] = _layer_norm(res, ln_ref[0:1, :], ln_ref[1:2, :])


def _token_mixer_layer(x, sh, sc, g, w_in, w_out, cw, vec, mu, ln, wgate, wlora, bd, ts):
    bsz, s, d = x.shape
    grid = (bsz, s // ts)
    full2 = lambda b, i: (0, 0)
    mod_spec = pl.BlockSpec((1, 1, d), lambda b, i: (b, 0, 0))
    f32 = jnp.float32

    def tile(n_cols):
        return pltpu.VMEM((ts, n_cols), f32)

    return pl.pallas_call(
        _mixer_kernel,
        grid=grid,
        in_specs=[
            pl.BlockSpec((1, ts, d), lambda b, i: (b, i, 0)),
            mod_spec, mod_spec, mod_spec,
            pl.BlockSpec(w_in.shape, full2),
            pl.BlockSpec(w_out.shape, full2),
            pl.BlockSpec(cw.shape, full2),
            pl.BlockSpec(vec.shape, full2),
            pl.BlockSpec(mu.shape, full2),
            pl.BlockSpec(ln.shape, full2),
            pl.BlockSpec(wgate.shape, full2),
            pl.BlockSpec(wlora.shape, full2),
            pl.BlockSpec(bd.shape, full2),
        ],
        out_specs=pl.BlockSpec((1, ts, d), lambda b, i: (b, i, 0)),
        out_shape=jax.ShapeDtypeStruct((bsz, s, d), f32),
        scratch_shapes=[
            pltpu.VMEM((CONF_HIST + ts, D_GRP), f32),
            pltpu.VMEM((SMALL_HIST + ts, D_GRP), f32),
            pltpu.VMEM((SMALL_HIST + ts, D_GRP), f32),
            pltpu.VMEM((SUBLANES, C_COLS), f32),
            pltpu.VMEM((SUBLANES, D_GRP), f32),
            pltpu.VMEM((N_HEADS, HEAD_DIM, HEAD_DIM), f32),
            tile(D_MODEL),
        ],
        compiler_params=pltpu.CompilerParams(
            dimension_semantics=("arbitrary", "arbitrary"),
            vmem_limit_bytes=VMEM_LIMIT_BYTES),
        name="token_mixers",
    )(x, sh, sc, g, w_in, w_out, cw, vec, mu, ln, wgate, wlora, bd)


def _first_index(mask, idx, sentinel):
    return jnp.min(jnp.where(mask, idx, sentinel), axis=0, keepdims=True)


def _router_kernel(x_ref, sh_ref, sc_ref, wr_ref, eb_ref, su_ref, sl_ref,
                   slot_ref, wt_ref, plan_ref):
    x = x_ref[0]
    u = x * (1.0 + sc_ref[0]) + sh_ref[0]
    tm = u.shape[0]
    logits = _dot_f32(wr_ref[...], u, ((1,), (1,)))
    scores = jax.nn.sigmoid(logits)
    biased = scores + eb_ref[...]
    neg_inf = jnp.float32(-jnp.inf)
    sub = lax.broadcasted_iota(jnp.int32, (E_PER_GROUP, tm), 0).astype(jnp.float32)

    groups = [biased[g * E_PER_GROUP:(g + 1) * E_PER_GROUP, :] for g in range(N_GROUPS)]
    gscore = []
    for blk in groups:
        m1 = jnp.max(blk, axis=0, keepdims=True)
        first = _first_index(blk == m1, sub, float(E_PER_GROUP))
        m2 = jnp.max(jnp.where(sub == first, neg_inf, blk), axis=0, keepdims=True)
        gscore.append(m1 + m2)
    gs = jnp.concatenate(gscore, axis=0)
    gidx = lax.broadcasted_iota(jnp.int32, (N_GROUPS, tm), 0).astype(jnp.float32)
    keep = jnp.zeros((N_GROUPS, tm), jnp.float32)
    for _ in range(TOPK_GROUPS):
        m = jnp.max(gs, axis=0, keepdims=True)
        first = _first_index(gs == m, gidx, float(N_GROUPS))
        sel = gidx == first
        keep = jnp.where(sel, 1.0, keep)
        gs = jnp.where(sel, neg_inf, gs)

    masked = [jnp.where(keep[g:g + 1, :] > 0.5, groups[g], neg_inf) for g in range(N_GROUPS)]
    eidx = [sub + float(g * E_PER_GROUP) for g in range(N_GROUPS)]
    chosen = [jnp.zeros((E_PER_GROUP, tm), jnp.float32) for _ in range(N_GROUPS)]
    for _ in range(TOP_K):
        m = functools.reduce(jnp.maximum, [jnp.max(b, axis=0, keepdims=True) for b in masked])
        first = functools.reduce(
            jnp.minimum,
            [_first_index(masked[g] == m, eidx[g], float(N_EXPERTS)) for g in range(N_GROUPS)])
        for g in range(N_GROUPS):
            sel = eidx[g] == first
            chosen[g] = jnp.where(sel, 1.0, chosen[g])
            masked[g] = jnp.where(sel, neg_inf, masked[g])

    picked = [jnp.where(chosen[g] > 0.5, scores[g * E_PER_GROUP:(g + 1) * E_PER_GROUP, :], 0.0)
              for g in range(N_GROUPS)]
    total = functools.reduce(jnp.add, [jnp.sum(p, axis=0, keepdims=True) for p in picked])
    wts = jnp.concatenate([p / total * ROUTED_SCALE for p in picked], axis=0)

    sel = jnp.concatenate(chosen, axis=0)
    sel_bf = _bf16(sel)
    rank = jnp.dot(sel_bf, su_ref[...], preferred_element_type=jnp.float32)
    cnt = jnp.sum(sel, axis=1, keepdims=True)
    off = _dot_f32(sl_ref[...].astype(jnp.float32), jnp.broadcast_to(cnt, (N_EXPERTS, LANES)))
    slot = off[:, 0:1] + rank
    kidx = jnp.dot(sl_ref[...], sel_bf, preferred_element_type=jnp.float32)
    slot_rows, wt_rows = [], []
    for kk in range(TOP_K):
        pick = (sel > 0.5) & (kidx == float(kk))
        slot_rows.append(jnp.sum(jnp.where(pick, slot, 0.0), axis=0, keepdims=True))
        wt_rows.append(jnp.sum(jnp.where(pick, wts, 0.0), axis=0, keepdims=True))
    lane_pad = [jnp.zeros((LANES - TOP_K, tm), jnp.float32)]
    slot_ref[0] = (jnp.concatenate(slot_rows + lane_pad, axis=0).T
                   * float(SLAB_ROWS)).astype(jnp.int32)
    wt_ref[0] = jnp.concatenate(wt_rows + lane_pad, axis=0).T
    e_sub = lax.broadcasted_iota(jnp.int32, (N_EXPERTS, LANES), 0)
    e_lane = lax.broadcasted_iota(jnp.int32, (N_EXPERTS, LANES), 1)
    diag = e_sub == e_lane
    off_row = jnp.sum(jnp.where(diag, off, 0.0), axis=0, keepdims=True)
    cnt_row = jnp.sum(jnp.where(diag, jnp.broadcast_to(cnt, (N_EXPERTS, LANES)), 0.0),
                      axis=0, keepdims=True)
    plan_ref[0] = jnp.concatenate([off_row, cnt_row], axis=0).astype(jnp.int32)


def _router_layer(x, sh, sc, w_router_t, e_bias_col, su, sl, tm):
    bsz, s, d = x.shape
    per_b = s // tm
    n_tiles = bsz * per_b
    mod_spec = pl.BlockSpec((1, 1, d), lambda b, i: (b, 0, 0))
    const2 = lambda b, i: (0, 0)
    tile3 = lambda b, i: (b * per_b + i, 0, 0)
    return pl.pallas_call(
        _router_kernel,
        grid=(bsz, per_b),
        in_specs=[
            pl.BlockSpec((1, tm, d), lambda b, i: (b, i, 0)),
            mod_spec, mod_spec,
            pl.BlockSpec(w_router_t.shape, const2),
            pl.BlockSpec(e_bias_col.shape, const2),
            pl.BlockSpec(su.shape, const2),
            pl.BlockSpec(sl.shape, const2),
        ],
        out_specs=[
            pl.BlockSpec((1, tm, LANES), tile3),
            pl.BlockSpec((1, tm, LANES), tile3),
            pl.BlockSpec((1, 2, LANES), tile3),
        ],
        out_shape=[
            jax.ShapeDtypeStruct((n_tiles, tm, LANES), jnp.int32),
            jax.ShapeDtypeStruct((n_tiles, tm, LANES), jnp.float32),
            jax.ShapeDtypeStruct((n_tiles, 2, LANES), jnp.int32),
        ],
        compiler_params=pltpu.CompilerParams(
            dimension_semantics=("arbitrary", "arbitrary"),
            vmem_limit_bytes=VMEM_LIMIT_BYTES),
        name="moe_router",
    )(x, sh, sc, w_router_t, e_bias_col, su, sl)


def _swiglu(u_bf16, w13, w2):
    hcat = jnp.dot(u_bf16, w13, preferred_element_type=jnp.float32)
    half = hcat.shape[1] // 2
    gate, up = hcat[:, :half], hcat[:, half:]
    act = gate * jax.nn.sigmoid(gate) * up
    return jnp.dot(_bf16(act), w2, preferred_element_type=jnp.float32)


N_DMA_QUEUES = 2
W_COPIES = 2 * N_DMA_QUEUES


def _weight_copies(layer, w13_hbm, w2_hbm, w13_buf, w2_buf, sems, e, slot):
    copies = []
    for hbm, buf in ((w13_hbm, w13_buf), (w2_hbm, w2_buf)):
        rows = hbm.shape[2] // N_DMA_QUEUES
        for part in range(N_DMA_QUEUES):
            sl = pl.ds(part * rows, rows)
            copies.append(pltpu.make_async_copy(hbm.at[layer, e, sl], buf.at[slot, sl],
                                                sems.at[len(copies), slot]))
    return copies


def _start_weight_copies(layer, w13_hbm, w2_hbm, w13_buf, w2_buf, sems, e, slot):
    for n, cp in enumerate(_weight_copies(layer, w13_hbm, w2_hbm, w13_buf, w2_buf, sems, e, slot)):
        cp.start(priority=n % N_DMA_QUEUES)


def _expert_kernel(layer, plan_ref, slot_ref, wt_ref, x_ref, sh_ref, sc_ref, g_ref, w13_hbm,
                   w2_hbm, ws13_ref, ws2_ref, ln_ref, o_ref, rows_f, xf, w13_buf, w2_buf, sems):
    i = pl.program_id(0)
    tm = x_ref.shape[0]
    n_exp = w13_hbm.shape[1]
    n_col = x_ref.shape[1] // LANES

    @pl.when(i == 0)
    def _clear():
        xf[...] = jnp.zeros(xf.shape, xf.dtype)

    for e0 in range(W_AHEAD):
        _start_weight_copies(layer, w13_hbm, w2_hbm, w13_buf, w2_buf, sems, e0, e0)

    u = x_ref[...] * (1.0 + sc_ref[0]) + sh_ref[0]
    for j in range(n_col):
        rows_f[pl.ds(j, tm, stride=SLAB_ROWS), :] = u[:, j * LANES:(j + 1) * LANES]

    def scatter(it, carry):
        for h in range(TOKEN_UNROLL):
            t = it * TOKEN_UNROLL + h
            row = rows_f[pl.ds(pl.multiple_of(t * SLAB_ROWS, SLAB_ROWS), SLAB_ROWS), :]
            for k in range(TOP_K):
                dst = pl.multiple_of(slot_ref[t * TOP_K + k], SLAB_ROWS)
                xf[pl.ds(dst, SLAB_ROWS), :] = row
        return carry

    lax.fori_loop(0, tm // TOKEN_UNROLL, scatter, 0)

    def expert(e, carry):
        slot = lax.rem(e, W_SLOTS)
        first = plan_ref[i * 2 * LANES + e]
        count = plan_ref[i * 2 * LANES + LANES + e]
        n_blocks = lax.div(count + (EXPERT_ROWS - 1), EXPERT_ROWS)

        def load_rows(j):
            base = (first + j * EXPERT_ROWS) * SLAB_ROWS
            old = [xf[pl.ds(base + q, EXPERT_ROWS, stride=SLAB_ROWS), :] for q in range(n_col)]
            return old, jnp.concatenate([_bf16(o) for o in old], axis=1)

        def transform_store(j, old, xb):
            base = (first + j * EXPERT_ROWS) * SLAB_ROWS
            y = _swiglu(xb, w13_buf[slot], w2_buf[slot])
            live = (lax.broadcasted_iota(jnp.int32, (EXPERT_ROWS, LANES), 0)
                    < count - j * EXPERT_ROWS)
            for q in range(n_col):
                xf[pl.ds(base + q, EXPERT_ROWS, stride=SLAB_ROWS), :] = jnp.where(
                    live, y[:, q * LANES:(q + 1) * LANES], old[q])

        old0, xb0 = load_rows(0)
        for cp in _weight_copies(layer, w13_hbm, w2_hbm, w13_buf, w2_buf, sems, e, slot):
            cp.wait()

        @pl.when(e + W_AHEAD < n_exp)
        def _prefetch():
            nxt = e + W_AHEAD
            _start_weight_copies(layer, w13_hbm, w2_hbm, w13_buf, w2_buf, sems, nxt,
                                 lax.rem(nxt, W_SLOTS))

        transform_store(0, old0, xb0)

        def block(j, c2):
            old, xb = load_rows(j)
            transform_store(j, old, xb)
            return c2

        lax.fori_loop(1, n_blocks, block, 0)
        return carry

    lax.fori_loop(0, n_exp, expert, 0)

    def gather(it, carry):
        for h in range(TOKEN_UNROLL):
            t = it * TOKEN_UNROLL + h
            acc = jnp.zeros((SLAB_ROWS, LANES), jnp.float32)
            for k in range(TOP_K):
                src = pl.multiple_of(slot_ref[t * TOP_K + k], SLAB_ROWS)
                acc = acc + wt_ref[t * TOP_K + k] * xf[pl.ds(src, SLAB_ROWS), :]
            rows_f[pl.ds(pl.multiple_of(t * SLAB_ROWS, SLAB_ROWS), SLAB_ROWS), :] = acc
        return carry

    lax.fori_loop(0, tm // TOKEN_UNROLL, gather, 0)
    routed = jnp.concatenate([rows_f[pl.ds(j, tm, stride=SLAB_ROWS), :] for j in range(n_col)],
                             axis=1)
    x = x_ref[...]
    u = x * (1.0 + sc_ref[0]) + sh_ref[0]
    ffn = routed + _swiglu(_bf16(u), ws13_ref[...], ws2_ref[...])
    res = ln_ref[2:3, :] * x + (1.0 + g_ref[0]) * ffn
    o_ref[...] = _layer_norm(res, ln_ref[0:1, :], ln_ref[1:2, :])


def _expert_layer(x, sh, sc, g, plan, slots, wts, layer, w13, w2, ws13, ws2, ln, tm):
    bsz, s, d = x.shape
    per_b = s // tm
    n_tiles = bsz * per_b
    mod_spec = pl.BlockSpec((1, 1, d), lambda i, plan: (i // per_b, 0, 0))
    full2 = lambda i, plan: (0, 0)
    once = pl.Buffered(1)
    tile_smem = pl.BlockSpec((TOP_K * tm,), lambda i, plan: (i,), memory_space=pltpu.SMEM)
    slab_rows = (TOP_K * tm + EXPERT_ROWS) * SLAB_ROWS
    grid_spec = pltpu.PrefetchScalarGridSpec(
        num_scalar_prefetch=1,
        grid=(n_tiles,),
        in_specs=[
            tile_smem, tile_smem,
            pl.BlockSpec((tm, d), lambda i, plan: (i, 0), pipeline_mode=once),
            mod_spec, mod_spec, mod_spec,
            pl.BlockSpec(memory_space=pl.ANY),
            pl.BlockSpec(memory_space=pl.ANY),
            pl.BlockSpec(ws13.shape, full2, pipeline_mode=once),
            pl.BlockSpec(ws2.shape, full2, pipeline_mode=once),
            pl.BlockSpec(ln.shape, full2),
        ],
        out_specs=pl.BlockSpec((tm, d), lambda i, plan: (i, 0), pipeline_mode=once),
        scratch_shapes=[
            pltpu.VMEM((tm * SLAB_ROWS, LANES), jnp.float32),
            pltpu.VMEM((slab_rows, LANES), jnp.float32),
            pltpu.VMEM((W_SLOTS,) + w13.shape[2:], w13.dtype),
            pltpu.VMEM((W_SLOTS,) + w2.shape[2:], w2.dtype),
            pltpu.SemaphoreType.DMA((W_COPIES, W_SLOTS)),
        ],
    )
    out = pl.pallas_call(
        functools.partial(_expert_kernel, layer),
        grid_spec=grid_spec,
        out_shape=jax.ShapeDtypeStruct((bsz * s, d), jnp.float32),
        compiler_params=pltpu.CompilerParams(
            dimension_semantics=("arbitrary",),
            vmem_limit_bytes=VMEM_LIMIT_BYTES),
        name="moe_experts",
    )(plan.reshape(-1), slots[:, :, :TOP_K].reshape(-1), wts[:, :, :TOP_K].reshape(-1),
      x.reshape(bsz * s, d), sh, sc, g,
      w13, w2, ws13, ws2, ln)
    return out.reshape(bsz, s, d)


def _block_diag(w):
    h, n, _ = w.shape
    eye = jnp.eye(h, dtype=w.dtype)
    return (eye[:, None, :, None] * w[:, :, None, :]).reshape(h * n, h * n)


def _pad_rows(a, n_rows):
    return jnp.concatenate([a, jnp.zeros((n_rows - a.shape[0], a.shape[1]), a.dtype)], axis=0)


def _tile_rows(seq_len, want):
    t = min(want, seq_len)
    assert seq_len % t == 0 and t % RWKV_CHUNK == 0
    return t


def kernel(x, c, w_mod, b_mod, w_in, w_out, conv_a, conv_a_bias, ln_a_g, ln_a_b, conv_b, conv_b_bias, w_rg, b_rg, w_ig, b_ig, lru_lambda, mu_c, w0, w_w2, a0, w_a2, w_g2, k_k, k_a, r_k, gn_g, gn_b, conv_d, ln1_g, ln1_b, w_router, e_bias, w13, w2, ws13, ws2, ln2_g, ln2_b):
    n_layers = w_mod.shape[0]
    bsz, s, d = x.shape
    alpha_dn = (2.0 * n_layers) ** 0.25
    ts_mix = _tile_rows(s, 512)
    tm_moe = _tile_rows(s, 1024)
    bf = jnp.bfloat16

    mod = _modulation(c, w_mod, b_mod)
    bd = _block_diag(jnp.ones((N_HEADS, HEAD_DIM, HEAD_DIM), bf))
    alpha_row = jnp.full((1, d), alpha_dn, jnp.float32)
    tri_tok = jnp.triu(jnp.ones((tm_moe, tm_moe), bf), k=1)
    tri_exp = jnp.tril(jnp.ones((N_EXPERTS, N_EXPERTS), bf), k=-1)
    w13_bf, w2_bf = w13.astype(bf), w2.astype(bf)

    for l in range(n_layers):
        sh1, sc1, g1, sh2, sc2, g2 = [mod[l, :, i * d:(i + 1) * d].reshape(bsz, 1, d)
                                      for i in range(6)]
        cw = _pad_rows(jnp.concatenate([conv_a[l], conv_b[l], conv_d[l]], axis=0), N_CW_ROWS)
        vec = _pad_rows(jnp.stack([
            conv_a_bias[l], ln_a_g[l], ln_a_b[l], conv_b_bias[l], b_rg[l], b_ig[l], lru_lambda[l],
            w0[l], a0[l], k_k[l], k_a[l], r_k[l].reshape(D_GRP), gn_g[l], gn_b[l]], axis=0),
            N_VEC_ROWS)
        wgate = jnp.concatenate([_block_diag(w_rg[l]), _block_diag(w_ig[l])], axis=1).astype(bf)
        wlora = jnp.zeros((LORA_W + LORA_A + LORA_G, 3 * D_GRP), jnp.float32)
        wlora = wlora.at[0:LORA_W, 0:D_GRP].set(w_w2[l])
        wlora = wlora.at[LORA_W:LORA_W + LORA_A, D_GRP:2 * D_GRP].set(w_a2[l])
        wlora = wlora.at[LORA_W + LORA_A:, 2 * D_GRP:].set(w_g2[l]).astype(bf)
        ln1 = jnp.concatenate([ln1_g[l][None], ln1_b[l][None], alpha_row], axis=0)
        ln2 = jnp.concatenate([ln2_g[l][None], ln2_b[l][None], alpha_row], axis=0)

        x = _token_mixer_layer(x, sh1, sc1, g1, w_in[l].astype(bf), w_out[l].astype(bf), cw, vec,
                               mu_c[l][None], ln1, wgate, wlora, bd, ts_mix)
        slots, wts, plan = _router_layer(x, sh2, sc2, w_router[l].T, e_bias[l][:, None],
                                         tri_tok, tri_exp, tm_moe)
        x = _expert_layer(x, sh2, sc2, g2, plan, slots, wts, l, w13_bf, w2_bf,
                          ws13[l].astype(bf), ws2[l].astype(bf), ln2, tm_moe)
    return x
```
